```python
import functools
import jax, jax.numpy as jnp
from jax import lax
import numpy as np

D_MODEL = 1024
BATCH = 16
SEQ = 2048
DEPTH = 1
DEC_BATCH = 128
DEC_SEQ = 1
PAST_LEN = 8192
PAGE_SIZE = 128

MIX_WIDTH = D_MODEL
A_WIDTH = MIX_WIDTH // 2
A_GROUPS = 4
A_GROUP_DIM = A_WIDTH // A_GROUPS
CHUNK = 128
B_WIDTH = MIX_WIDTH - A_WIDTH
HEAD_DIM = 64
N_HEADS = B_WIDTH // HEAD_DIM
N_KV_HEADS = 2
GROUP_SIZE = N_HEADS // N_KV_HEADS
N_IDX_HEADS = 4
IDX_DIM = 64
IDX_W_SCALE = (N_IDX_HEADS * IDX_DIM) ** -0.5
TOPK_MAX = 256
Q_BLOCK = 128
ROPE_THETA = 10000.0
D_FF = 2816
LN_EPS = 1e-5
ALPHA = (2.0 * DEPTH) ** 0.25
BETA = (8.0 * DEPTH) ** -0.25
IN_SPLITS = (A_WIDTH, A_WIDTH, N_HEADS * HEAD_DIM, N_KV_HEADS * HEAD_DIM, N_KV_HEADS * HEAD_DIM, N_IDX_HEADS * IDX_DIM, IDX_DIM, N_IDX_HEADS)
IN_WIDTH = sum(IN_SPLITS)
SPLIT_POINTS = tuple(sum(IN_SPLITS[:i + 1]) for i in range(len(IN_SPLITS) - 1))

kernel_name = 'hymba_gmlp_dsa_macaron_deepnorm_step'


def _layernorm(x, g, b):
    xf = x.astype(jnp.float32)
    mu = jnp.mean(xf, axis=-1, keepdims=True)
    var = jnp.mean(jnp.square(xf - mu), axis=-1, keepdims=True)
    return ((xf - mu) * lax.rsqrt(var + LN_EPS) * g.astype(jnp.float32) + b.astype(jnp.float32)).astype(x.dtype)


def _swiglu(x, w_up, w_down):
    gate, up = jnp.split(x @ w_up, 2, axis=-1)
    return (jax.nn.silu(gate) * up) @ w_down


def _rope(x, pos):
    half = x.shape[-1] // 2
    inv = ROPE_THETA ** (-jnp.arange(half, dtype=jnp.float32) / half)
    ang = pos.astype(jnp.float32)[:, None] * inv[None, :]
    cos = jnp.cos(ang)[None, :, None, :]
    sin = jnp.sin(ang)[None, :, None, :]
    xf = x.astype(jnp.float32)
    x1, x2 = xf[..., :half], xf[..., half:]
    return jnp.concatenate([x1 * cos - x2 * sin, x2 * cos + x1 * sin], axis=-1).astype(x.dtype)


def _mixer_inputs(h, w_in, pos):
    B, T, _ = h.shape
    u, va, q, k, v, qi, ki, wi = jnp.split(h @ w_in, SPLIT_POINTS, axis=-1)
    q = _rope(q.reshape(B, T, N_HEADS, HEAD_DIM), pos)
    k = _rope(k.reshape(B, T, N_KV_HEADS, HEAD_DIM), pos)
    v = v.reshape(B, T, N_KV_HEADS, HEAD_DIM)
    qi = _rope(qi.reshape(B, T, N_IDX_HEADS, IDX_DIM), pos)
    ki = _rope(ki.reshape(B, T, 1, IDX_DIM), pos)[:, :, 0]
    return u, va, q, k, v, qi, ki, wi * IDX_W_SCALE


def _chunk_gmlp(u, v, a_ln_g, a_ln_b, a_ws, a_bs):
    B, T, _ = u.shape
    u = jax.nn.gelu(u)
    v = _layernorm(jax.nn.gelu(v), a_ln_g, a_ln_b)
    n_chunks = -(-T // CHUNK)
    pad = n_chunks * CHUNK - T
    vp = jnp.pad(v, ((0, 0), (0, pad), (0, 0))).reshape(B, n_chunks, CHUNK, A_GROUPS, A_GROUP_DIM)
    causal = jnp.tril(jnp.ones((CHUNK, CHUNK), dtype=bool))
    ws = jnp.where(causal[None], a_ws, 0)
    mixed = jnp.einsum('gts,bcsgd->bctgd', ws, vp) + a_bs.T[None, None, :, :, None]
    mixed = mixed.reshape(B, n_chunks * CHUNK, A_WIDTH)[:, :T]
    return u * mixed, v


def _indexer_scores(q_idx, w_idx, k_idx, q_pos):
    dots = jnp.einsum('bthd,bsd->bths', q_idx.astype(jnp.float32), k_idx.astype(jnp.float32))
    s = jnp.einsum('bth,bths->bts', w_idx.astype(jnp.float32), jax.nn.relu(dots))
    visible = jnp.arange(k_idx.shape[1])[None, :] <= q_pos[:, None]
    return jnp.where(visible[None], s, -jnp.inf)


def _select(scores, q_pos, topk):
    _, idx = lax.top_k(scores, topk)
    return idx, idx <= q_pos[None, :, None]


def _sparse_attend(q, k_sel, v_sel, valid):
    B, T = q.shape[:2]
    qg = q.reshape(B, T, N_KV_HEADS, GROUP_SIZE, HEAD_DIM).astype(jnp.float32)
    logits = jnp.einsum('btgrd,btkgd->btgrk', qg, k_sel.astype(jnp.float32)) * (HEAD_DIM ** -0.5)
    logits = jnp.where(valid[:, :, None, None, :], logits, -jnp.inf)
    p = jax.nn.softmax(logits, axis=-1)
    out = jnp.einsum('btgrk,btkgd->btgrd', p, v_sel.astype(jnp.float32))
    return out.reshape(B, T, N_HEADS * HEAD_DIM).astype(q.dtype)


def _prompt_attention(q, k, v, q_idx, k_idx, w_idx, pos):
    B, S = q.shape[:2]
    topk = min(TOPK_MAX, S // 4)
    nb = S // Q_BLOCK
    b_ix = jnp.arange(B)[:, None, None]

    def blockify(t):
        return jnp.moveaxis(t.reshape((B, nb, Q_BLOCK) + t.shape[2:]), 1, 0)

    def one_block(args):
        qb, qib, wb, pb = args
        idx, valid = _select(_indexer_scores(qib, wb, k_idx, pb), pb, topk)
        return _sparse_attend(qb, k[b_ix, idx], v[b_ix, idx], valid)

    out = lax.map(one_block, (blockify(q), blockify(q_idx), blockify(w_idx), pos.reshape(nb, Q_BLOCK)))
    return jnp.moveaxis(out, 0, 1).reshape(B, S, N_HEADS * HEAD_DIM)


def _sample_attention(q, k_new, v_new, q_idx, k_idx_new, w_idx, pos, cache_k, cache_v, cache_kidx, page_table):
    DB, T = q.shape[:2]
    past = page_table.shape[1] * PAGE_SIZE
    topk = min(TOPK_MAX, (past + T) // 4)
    kidx_past = cache_kidx[page_table].reshape(DB, past, IDX_DIM)
    kidx_all = jnp.concatenate([kidx_past, k_idx_new], axis=1)
    idx, valid = _select(_indexer_scores(q_idx, w_idx, kidx_all, pos), pos, topk)
    b_ix = jnp.arange(DB)[:, None, None]
    in_past = (idx < past)[..., None, None]
    p_pos = jnp.minimum(idx, past - 1)
    phys = page_table[b_ix, p_pos // PAGE_SIZE]
    slot = p_pos % PAGE_SIZE
    n_pos = jnp.clip(idx - past, 0, T - 1)
    k_sel = jnp.where(in_past, cache_k[phys, slot], k_new[b_ix, n_pos])
    v_sel = jnp.where(in_past, cache_v[phys, slot], v_new[b_ix, n_pos])
    return _sparse_attend(q, k_sel, v_sel, valid)


def _layer(x, pos, attend, ln1_g, ln1_b, ffn1_w_up, ffn1_w_down, ln2_g, ln2_b, w_in, a_ln_g, a_ln_b, a_ws, a_bs, w_out, ln3_g, ln3_b, ffn2_w_up, ffn2_w_down):
    x = _layernorm(ALPHA * x + 0.5 * _swiglu(x, ffn1_w_up, ffn1_w_down), ln1_g, ln1_b)
    u, va, q, k, v, qi, ki, wi = _mixer_inputs(x, w_in, pos)
    a_out, a_v = _chunk_gmlp(u, va, a_ln_g, a_ln_b, a_ws, a_bs)
    b_out = attend(q, k, v, qi, ki, wi, pos)
    mix = jnp.concatenate([a_out, b_out], axis=-1) @ w_out
    x = _layernorm(ALPHA * x + mix, ln2_g, ln2_b)
    x = _layernorm(ALPHA * x + 0.5 * _swiglu(x, ffn2_w_up, ffn2_w_down), ln3_g, ln3_b)
    return x, k, v, ki, a_v


def setup_inputs(seed: int = 0) -> dict:
    key = jax.random.key(seed)
    ks = jax.random.split(key, 26)
    n_pages = PAST_LEN // PAGE_SIZE
    n_pool = (5 * DEC_BATCH * n_pages) // 4
    f32 = jnp.float32

    def nrm(k, shape, scale=1.0):
        return jax.random.normal(k, shape, f32) * scale

    perm = jax.random.permutation(ks[5], n_pool)[:DEC_BATCH * n_pages]
    return {
        'x_prompt': nrm(ks[0], (BATCH, SEQ, D_MODEL)),
        'x_sample': nrm(ks[1], (DEC_BATCH, DEC_SEQ, D_MODEL)),
        'cache_k': nrm(ks[2], (DEPTH, n_pool, PAGE_SIZE, N_KV_HEADS, HEAD_DIM)),
        'cache_v': nrm(ks[3], (DEPTH, n_pool, PAGE_SIZE, N_KV_HEADS, HEAD_DIM)),
        'cache_kidx': nrm(ks[4], (DEPTH, n_pool, PAGE_SIZE, IDX_DIM)),
        'page_table': perm.reshape(DEC_BATCH, n_pages).astype(jnp.int32),
        'ln1_g': 1.0 + nrm(ks[6], (DEPTH, D_MODEL), 0.02),
        'ln1_b': nrm(ks[7], (DEPTH, D_MODEL), 0.02),
        'ffn1_w_up': nrm(ks[8], (DEPTH, D_MODEL, 2 * D_FF), D_MODEL ** -0.5),
        'ffn1_w_down': nrm(ks[9], (DEPTH, D_FF, D_MODEL), BETA * D_FF ** -0.5),
        'ln2_g': 1.0 + nrm(ks[10], (DEPTH, D_MODEL), 0.02),
        'ln2_b': nrm(ks[11], (DEPTH, D_MODEL), 0.02),
        'w_in': nrm(ks[12], (DEPTH, D_MODEL, IN_WIDTH), D_MODEL ** -0.5),
        'a_ln_g': 1.0 + nrm(ks[13], (DEPTH, A_WIDTH), 0.02),
        'a_ln_b': nrm(ks[14], (DEPTH, A_WIDTH), 0.02),
        'a_ws': nrm(ks[15], (DEPTH, A_GROUPS, CHUNK, CHUNK), CHUNK ** -0.5),
        'a_bs': 1.0 + nrm(ks[16], (DEPTH, A_GROUPS, CHUNK), 0.02),
        'w_out': nrm(ks[17], (DEPTH, MIX_WIDTH, D_MODEL), BETA * MIX_WIDTH ** -0.5),
        'ln3_g': 1.0 + nrm(ks[18], (DEPTH, D_MODEL), 0.02),
        'ln3_b': nrm(ks[19], (DEPTH, D_MODEL), 0.02),
        'ffn2_w_up': nrm(ks[20], (DEPTH, D_MODEL, 2 * D_FF), D_MODEL ** -0.5),
        'ffn2_w_down': nrm(ks[21], (DEPTH, D_FF, D_MODEL), BETA * D_FF ** -0.5),
    }


def reference(x_prompt, x_sample, cache_k, cache_v, cache_kidx, page_table, ln1_g, ln1_b, ffn1_w_up, ffn1_w_down, ln2_g, ln2_b, w_in, a_ln_g, a_ln_b, a_ws, a_bs, w_out, ln3_g, ln3_b, ffn2_w_up, ffn2_w_down):
    pos_p = jnp.arange(x_prompt.shape[1], dtype=jnp.int32)
    past = page_table.shape[1] * PAGE_SIZE
    pos_s = past + jnp.arange(x_sample.shape[1], dtype=jnp.int32)
    weights = (ln1_g, ln1_b, ffn1_w_up, ffn1_w_down, ln2_g, ln2_b, w_in, a_ln_g, a_ln_b, a_ws, a_bs, w_out, ln3_g, ln3_b, ffn2_w_up, ffn2_w_down)
    yp, ys = x_prompt, x_sample
    kp, vp, kip, ksm, vsm, kis, avs = [], [], [], [], [], [], []
    for l in range(DEPTH):
        lw = [w[l] for w in weights]
        yp, k1, v1, ki1, _ = _layer(yp, pos_p, _prompt_attention, *lw)
        attend_s = functools.partial(_sample_attention, cache_k=cache_k[l], cache_v=cache_v[l], cache_kidx=cache_kidx[l], page_table=page_table)
        ys, k2, v2, ki2, av2 = _layer(ys, pos_s, attend_s, *lw)
        kp.append(k1); vp.append(v1); kip.append(ki1)
        ksm.append(k2); vsm.append(v2); kis.append(ki2); avs.append(av2)
    return (yp, ys, jnp.stack(kp), jnp.stack(vp), jnp.stack(kip), jnp.stack(ksm), jnp.stack(vsm), jnp.stack(kis), jnp.stack(avs))
```

```python
import functools

import jax
import jax.numpy as jnp
from jax import lax
from jax.experimental import pallas as pl
from jax.experimental.pallas import tpu as pltpu

F32 = jnp.float32
BF16 = jnp.bfloat16

D_MODEL = 1024
D_FF = 2816
A_WIDTH = 512
A_GROUPS = 4
CHUNK = 128
HEAD_DIM = 64
N_HEADS = 8
N_KV_HEADS = 2
GROUP_SIZE = N_HEADS // N_KV_HEADS
N_IDX_HEADS = 4
IDX_DIM = 64
IDX_W_SCALE = (N_IDX_HEADS * IDX_DIM) ** -0.5
TOPK_MAX = 256
PAGE_SIZE = 128
ROPE_THETA = 10000.0
LN_EPS = 1e-5
DEPTH = 1
ALPHA = (2.0 * DEPTH) ** 0.25
ATTN_SCALE = HEAD_DIM ** -0.5

IN_SPLITS = (512, 512, 512, 128, 128, 256, 64, 4)
IN_WIDTH = sum(IN_SPLITS)
LANES = 128
IN_PAD = -(-IN_WIDTH // LANES) * LANES
COL_Q = 1024
COL_K = 1536
COL_V = 1664
COL_QI = 1792
COL_KIWI = 2048

FF_CHUNK = 256
N_FF_CHUNKS = D_FF // FF_CHUNK
VMEM_LIMIT = 56 * 1024 * 1024
INT_MIN = -2 ** 31
NEG_INF = float("-inf")


def _layernorm(y, g, b):
    mu = jnp.mean(y, axis=-1, keepdims=True)
    d = y - mu
    var = jnp.mean(d * d, axis=-1, keepdims=True)
    return d * lax.rsqrt(var + LN_EPS) * g + b


def _gelu_tanh(x):
    return 0.5 * x * (1.0 + jnp.tanh(0.7978845608028654 * (x + 0.044715 * (x * x * x))))


def _ffn_ln_kernel(x_ref, wgu_ref, wd_ref, g_ref, b_ref, o_ref):
    x = x_ref[...]
    xb = x.astype(BF16)
    acc = jnp.zeros(x.shape, F32)
    for c in range(N_FF_CHUNKS):
        gu = jnp.dot(xb, wgu_ref[c], preferred_element_type=F32)
        gate = gu[:, :FF_CHUNK]
        up = gu[:, FF_CHUNK:]
        h = (gate * (1.0 / (1.0 + jnp.exp(-gate))) * up).astype(BF16)
        acc = acc + jnp.dot(h, wd_ref[c], preferred_element_type=F32)
    o_ref[...] = _layernorm(ALPHA * x + 0.5 * acc, g_ref[...], b_ref[...])


def _ffn_ln(x, wgu, wd, g, b, tm):
    n = x.shape[0]
    const3 = lambda i: (0, 0, 0)
    const2 = lambda i: (0, 0)
    return pl.pallas_call(
        _ffn_ln_kernel,
        grid=(n // tm,),
        in_specs=[
            pl.BlockSpec((tm, D_MODEL), lambda i: (i, 0)),
            pl.BlockSpec((N_FF_CHUNKS, D_MODEL, 2 * FF_CHUNK), const3),
            pl.BlockSpec((N_FF_CHUNKS, FF_CHUNK, D_MODEL), const3),
            pl.BlockSpec((1, D_MODEL), const2),
            pl.BlockSpec((1, D_MODEL), const2),
        ],
        out_specs=pl.BlockSpec((tm, D_MODEL), lambda i: (i, 0)),
        out_shape=jax.ShapeDtypeStruct((n, D_MODEL), F32),
        compiler_params=pltpu.CompilerParams(
            dimension_semantics=("arbitrary",), vmem_limit_bytes=VMEM_LIMIT),
        name="ffn_ln",
    )(x, wgu, wd, g, b)


def _mixer_in_kernel(x_ref, w_ref, cos_ref, sin_ref, cosk_ref, sink_ref, ag_ref, ab_ref, ws_ref,
                     bias_ref, aout_ref, q_ref, qi_ref, kiwi_ref, kt_ref, vt_ref, kit_ref, *maybe_av_ref):
    tm = x_ref.shape[0]
    xb = x_ref[...].astype(BF16)

    def proj(lo, hi):
        return jnp.dot(xb, w_ref[:, lo:hi], preferred_element_type=F32)

    u = _gelu_tanh(proj(0, A_WIDTH))
    vn = _layernorm(_gelu_tanh(proj(A_WIDTH, 2 * A_WIDTH)), ag_ref[...], ab_ref[...])
    for av_ref in maybe_av_ref:
        av_ref[...] = vn
    vnb = vn.astype(BF16)
    for c in range(tm // CHUNK):
        rows = slice(c * CHUNK, (c + 1) * CHUNK)
        for g in range(A_GROUPS):
            cols = slice(g * LANES, (g + 1) * LANES)
            mixed = jnp.dot(ws_ref[g], vnb[rows, cols], preferred_element_type=F32) + bias_ref[:, cols]
            aout_ref[rows, cols] = (u[rows, cols] * mixed).astype(BF16)

    lane = lax.broadcasted_iota(jnp.int32, (tm, LANES), 1)
    first_half = (lane & (HEAD_DIM - 1)) < (HEAD_DIM // 2)

    def rope(x, cos, sin):
        partner = jnp.where(first_half, pltpu.roll(x, LANES - HEAD_DIM // 2, 1),
                            pltpu.roll(x, HEAD_DIM // 2, 1))
        return x * cos + partner * sin

    cos = cos_ref[...]
    sin = sin_ref[...]
    qall = proj(COL_Q, COL_K)
    for j in range(4):
        cols = slice(j * LANES, (j + 1) * LANES)
        q_ref[:, cols] = (rope(qall[:, cols], cos, sin) * ATTN_SCALE).astype(BF16)
    rest = proj(COL_K, IN_PAD)
    k = rope(rest[:, 0:128], cos, sin)
    kt_ref[0] = k.T.reshape(N_KV_HEADS, HEAD_DIM, tm)
    vt_ref[0] = rest[:, 128:256].T.reshape(N_KV_HEADS, HEAD_DIM, tm)
    for j in range(2):
        cols = slice(j * LANES, (j + 1) * LANES)
        qi_ref[:, cols] = rope(rest[:, 256 + j * LANES:256 + (j + 1) * LANES], cos, sin).astype(BF16)
    kiwi = rope(rest[:, 512:640], cosk_ref[...], sink_ref[...])
    kiwi_ref[...] = kiwi
    kit_ref[0] = kiwi.T[0:IDX_DIM]


def _mixer_in(x, w_in, cos, sin, cosk, sink, ag, ab, ws, bias, batch, seq, tm, emit_av):
    n = batch * seq
    nj = seq // tm
    row = lambda b, j: (b * nj + j, 0)
    tab = lambda b, j: (j, 0)
    const2 = lambda b, j: (0, 0)
    const3 = lambda b, j: (0, 0, 0)
    out_shape = [
        jax.ShapeDtypeStruct((n, A_WIDTH), BF16),
        jax.ShapeDtypeStruct((n, N_HEADS * HEAD_DIM), BF16),
        jax.ShapeDtypeStruct((n, N_IDX_HEADS * IDX_DIM), BF16),
        jax.ShapeDtypeStruct((n, LANES), F32),
        jax.ShapeDtypeStruct((batch, N_KV_HEADS, HEAD_DIM, seq), F32),
        jax.ShapeDtypeStruct((batch, N_KV_HEADS, HEAD_DIM, seq), F32),
        jax.ShapeDtypeStruct((batch, IDX_DIM, seq), F32),
    ]
    out_specs = [
        pl.BlockSpec((tm, A_WIDTH), row),
        pl.BlockSpec((tm, N_HEADS * HEAD_DIM), row),
        pl.BlockSpec((tm, N_IDX_HEADS * IDX_DIM), row),
        pl.BlockSpec((tm, LANES), row),
        pl.BlockSpec((1, N_KV_HEADS, HEAD_DIM, tm), lambda b, j: (b, 0, 0, j)),
        pl.BlockSpec((1, N_KV_HEADS, HEAD_DIM, tm), lambda b, j: (b, 0, 0, j)),
        pl.BlockSpec((1, IDX_DIM, tm), lambda b, j: (b, 0, j)),
    ]
    if emit_av:
        out_shape.append(jax.ShapeDtypeStruct((n, A_WIDTH), F32))
        out_specs.append(pl.BlockSpec((tm, A_WIDTH), row))
    return pl.pallas_call(
        _mixer_in_kernel,
        grid=(batch, nj),
        in_specs=[
            pl.BlockSpec((tm, D_MODEL), row),
            pl.BlockSpec((D_MODEL, IN_PAD), const2),
            pl.BlockSpec((tm, LANES), tab),
            pl.BlockSpec((tm, LANES), tab),
            pl.BlockSpec((tm, LANES), tab),
            pl.BlockSpec((tm, LANES), tab),
            pl.BlockSpec((1, A_WIDTH), const2),
            pl.BlockSpec((1, A_WIDTH), const2),
            pl.BlockSpec((A_GROUPS, CHUNK, CHUNK), const3),
            pl.BlockSpec((CHUNK, A_WIDTH), const2),
        ],
        out_specs=out_specs,
        out_shape=out_shape,
        compiler_params=pltpu.CompilerParams(
            dimension_semantics=("arbitrary", "arbitrary"), vmem_limit_bytes=VMEM_LIMIT),
        name="mixer_in",
    )(x, w_in, cos, sin, cosk, sink, ag, ab, ws, bias)


def _sortable_key(s):
    s = jnp.where(s == 0.0, 0.0, s)
    bits = pltpu.bitcast(s, jnp.int32)
    return jnp.where(bits < 0, bits ^ jnp.int32(0x7FFFFFFF), bits)


def _kth_largest_key(count_ge, rows, k):
    kf = jnp.float32(k)
    zero = jnp.zeros((rows, 1), jnp.int32)
    cur = jnp.where(count_ge(zero) >= kf, zero, jnp.int32(INT_MIN))

    def body(it, cur):
        cand = cur | jnp.left_shift(jnp.int32(1), jnp.int32(30) - it)
        return jnp.where(count_ge(cand) >= kf, cand, cur)

    return lax.fori_loop(0, 31, body, cur)


def _upper_tri_ones():
    r = lax.broadcasted_iota(jnp.int32, (LANES, LANES), 0)
    c = lax.broadcasted_iota(jnp.int32, (LANES, LANES), 1)
    return jnp.where(r <= c, 1.0, 0.0).astype(BF16)


def _tie_break_bias(key_ref, bias_ref, tau, need, n_cols):
    tri = _upper_tri_ones()
    rows = tau.shape[0]
    run = jnp.zeros((rows, 1), F32)
    for c in range(n_cols // LANES):
        cols = slice(c * LANES, (c + 1) * LANES)
        kc = key_ref[:, cols]
        eq = jnp.where(kc == tau, 1.0, 0.0)
        rank = run + jnp.dot(eq.astype(BF16), tri, preferred_element_type=F32)
        keep = (kc > tau) | ((kc == tau) & (rank <= need))
        bias_ref[:, cols] = jnp.where(keep, 0.0, NEG_INF)
        run = run + jnp.sum(eq, axis=-1, keepdims=True)
    return run


Q_BLOCK = 128
ATTN_CLASSES = ((256, False), (512, True), (1024, True), (1536, True), (2048, True))


def _split_heads_rows(x_ref_val):
    rows = x_ref_val.shape[0]
    lane = lax.broadcasted_iota(jnp.int32, (rows, LANES), 1)
    lo = lane < HEAD_DIM
    parts = []
    for j in range(x_ref_val.shape[1] // LANES):
        blk = x_ref_val[:, j * LANES:(j + 1) * LANES].astype(F32)
        parts.append(jnp.where(lo, blk, 0.0).astype(BF16))
        parts.append(jnp.where(lo, 0.0, blk).astype(BF16))
    return jnp.concatenate(parts, axis=0)


def _prompt_attn_block(n_keys, search, i, q_ref, qi_ref, kiwi_ref, kit_ref, kt_ref, vt_ref, o_ref,
                       key_ref, bias_ref):
    tq = Q_BLOCK
    qi4 = _split_heads_rows(qi_ref[...])
    kit = kit_ref[0, :, 0:n_keys].astype(BF16)
    dots = jnp.dot(qi4, jnp.concatenate([kit, kit], axis=0), preferred_element_type=F32)
    wq = kiwi_ref[...]
    s = jnp.zeros((tq, n_keys), F32)
    for h in range(N_IDX_HEADS):
        s = s + wq[:, IDX_DIM + h:IDX_DIM + h + 1] * jnp.maximum(dots[h * tq:(h + 1) * tq], 0.0)
    col = lax.broadcasted_iota(jnp.int32, (tq, n_keys), 1)
    row = i * tq + lax.broadcasted_iota(jnp.int32, (tq, n_keys), 0)
    visible = col <= row

    if not search:
        bias_ref[:, 0:n_keys] = jnp.where(visible, 0.0, NEG_INF)
    else:
        key_ref[:, 0:n_keys] = _sortable_key(jnp.where(visible, s, NEG_INF))

        def count_ge(c):
            return jnp.sum(jnp.where(key_ref[:, 0:n_keys] >= c, 1.0, 0.0), axis=-1, keepdims=True)

        tau = _kth_largest_key(count_ge, tq, TOPK_MAX)
        key = key_ref[:, 0:n_keys]
        bias_ref[:, 0:n_keys] = jnp.where(key >= tau, 0.0, NEG_INF)
        n_gt = jnp.sum(jnp.where(key > tau, 1.0, 0.0), axis=-1, keepdims=True)
        n_ge = jnp.sum(jnp.where(key >= tau, 1.0, 0.0), axis=-1, keepdims=True)
        need = jnp.float32(TOPK_MAX) - n_gt

        @pl.when(jnp.max(n_ge) > jnp.float32(TOPK_MAX))
        def _():
            _tie_break_bias(key_ref, bias_ref, tau, need, n_keys)

    bias = bias_ref[:, 0:n_keys]
    lane = lax.broadcasted_iota(jnp.int32, (tq, LANES), 1)
    lo = lane < HEAD_DIM
    for g in range(N_KV_HEADS):
        q4 = _split_heads_rows(q_ref[:, g * 256:(g + 1) * 256])
        kt = kt_ref[0, g, :, 0:n_keys].astype(BF16)
        vt = vt_ref[0, g, :, 0:n_keys].astype(BF16)
        logits = jnp.dot(q4, jnp.concatenate([kt, kt], axis=0), preferred_element_type=F32)
        vdup = jnp.concatenate([vt, vt], axis=0)
        outs = []
        for r in range(GROUP_SIZE):
            lg = logits[r * tq:(r + 1) * tq] + bias
            m = jnp.max(lg, axis=-1, keepdims=True)
            p = jnp.exp(lg - m)
            l = jnp.sum(p, axis=-1, keepdims=True)
            pv = lax.dot_general(p.astype(BF16), vdup, (((1,), (1,)), ((), ())),
                                 preferred_element_type=F32)
            outs.append(pv * (1.0 / l))
        for j in range(2):
            o_ref[:, g * 256 + j * LANES:g * 256 + (j + 1) * LANES] = jnp.where(
                lo, outs[2 * j], outs[2 * j + 1]).astype(BF16)


def _prompt_attn_kernel(q_ref, qi_ref, kiwi_ref, kit_ref, kt_ref, vt_ref, o_ref, key_ref, bias_ref):
    i = pl.program_id(1)
    lo_blk = 0
    for n_keys, search in ATTN_CLASSES:
        hi_blk = n_keys // Q_BLOCK

        @pl.when((i >= lo_blk) & (i < hi_blk))
        def _(n_keys=n_keys, search=search):
            _prompt_attn_block(n_keys, search, i, q_ref, qi_ref, kiwi_ref, kit_ref, kt_ref, vt_ref,
                               o_ref, key_ref, bias_ref)

        lo_blk = hi_blk


def _prompt_attention(q, qi, kiwi, kit, kt, vt, batch, seq):
    nb = seq // Q_BLOCK
    row = lambda b, i: (b * nb + i, 0)
    return pl.pallas_call(
        _prompt_attn_kernel,
        grid=(batch, nb),
        in_specs=[
            pl.BlockSpec((Q_BLOCK, N_HEADS * HEAD_DIM), row),
            pl.BlockSpec((Q_BLOCK, N_IDX_HEADS * IDX_DIM), row),
            pl.BlockSpec((Q_BLOCK, LANES), row),
            pl.BlockSpec((1, IDX_DIM, seq), lambda b, i: (b, 0, 0)),
            pl.BlockSpec((1, N_KV_HEADS, HEAD_DIM, seq), lambda b, i: (b, 0, 0, 0)),
            pl.BlockSpec((1, N_KV_HEADS, HEAD_DIM, seq), lambda b, i: (b, 0, 0, 0)),
        ],
        out_specs=pl.BlockSpec((Q_BLOCK, N_HEADS * HEAD_DIM), row),
        out_shape=jax.ShapeDtypeStruct((batch * seq, N_HEADS * HEAD_DIM), BF16),
        scratch_shapes=[pltpu.VMEM((Q_BLOCK, seq), jnp.int32), pltpu.VMEM((Q_BLOCK, seq), F32)],
        compiler_params=pltpu.CompilerParams(
            dimension_semantics=("arbitrary", "arbitrary"), vmem_limit_bytes=VMEM_LIMIT),
        name="prompt_attention",
    )(q, qi, kiwi, kit, kt, vt)


def _page_copies(pt_ref, b, n_pages, src_hbm, dst_for_page, sem):
    return [pltpu.make_async_copy(src_hbm.at[pt_ref[b * n_pages + p]], dst_for_page(p), sem)
            for p in range(n_pages)]


def _sample_scores_kernel(pt_ref, a_ref, w_ref, kidx_hbm, o_ref, buf, sem, *, n_pages):
    b = pl.program_id(0)
    nb = pl.num_programs(0)
    slot = b % 2

    def copies(bb, sl):
        return _page_copies(pt_ref, bb, n_pages, kidx_hbm,
                            lambda p: buf.at[sl, :, pl.ds(p * PAGE_SIZE, PAGE_SIZE)], sem.at[sl])

    @pl.when(b == 0)
    def _():
        for cp in copies(b, slot):
            cp.start()

    @pl.when(b + 1 < nb)
    def _():
        for cp in copies(b + 1, 1 - slot):
            cp.start()

    for cp in copies(b, slot):
        cp.wait()
    dots = jnp.dot(a_ref[0], buf[slot].astype(BF16), preferred_element_type=F32)
    o_ref[0] = jnp.sum(jnp.maximum(dots, 0.0) * w_ref[0], axis=0, keepdims=True)


def _sample_scores(pt_flat, a, w, kidx_t, n_samples, n_pages):
    past = n_pages * PAGE_SIZE
    return pl.pallas_call(
        functools.partial(_sample_scores_kernel, n_pages=n_pages),
        grid_spec=pltpu.PrefetchScalarGridSpec(
            num_scalar_prefetch=1,
            grid=(n_samples,),
            in_specs=[
                pl.BlockSpec((1, 16, IDX_DIM), lambda b, pt: (b, 0, 0)),
                pl.BlockSpec((1, 16, 1), lambda b, pt: (b, 0, 0)),
                pl.BlockSpec(memory_space=pl.ANY),
            ],
            out_specs=pl.BlockSpec((1, 1, past), lambda b, pt: (b, 0, 0)),
            scratch_shapes=[pltpu.VMEM((2, IDX_DIM, past), F32), pltpu.SemaphoreType.DMA((2,))],
        ),
        out_shape=jax.ShapeDtypeStruct((n_samples, 1, past), F32),
        compiler_params=pltpu.CompilerParams(
            dimension_semantics=("arbitrary",), vmem_limit_bytes=VMEM_LIMIT),
        name="sample_scores",
    )(pt_flat, a, w, kidx_t)


def _sample_select_kernel(s_ref, qi_ref, kiwi_ref, bias_ref, bias_new_ref, key_ref, *, topk):
    rows, past = s_ref.shape
    lane = lax.broadcasted_iota(jnp.int32, (rows, LANES), 1)
    lo = lane < IDX_DIM
    kiwi = kiwi_ref[...]
    kib = kiwi.astype(BF16).astype(F32)
    kidup = jnp.where(lo, kib, pltpu.roll(kib, IDX_DIM, 1))
    s_new = jnp.zeros((rows, 1), F32)
    for j in range(2):
        t = qi_ref[:, j * LANES:(j + 1) * LANES].astype(F32) * kidup
        for half in range(2):
            d = jnp.sum(jnp.where(lo if half == 0 else jnp.logical_not(lo), t, 0.0), axis=-1, keepdims=True)
            h = 2 * j + half
            s_new = s_new + kiwi[:, IDX_DIM + h:IDX_DIM + h + 1] * jnp.maximum(d, 0.0)
    key_new = _sortable_key(s_new)
    key_ref[...] = _sortable_key(s_ref[...])

    def count_ge(c):
        return (jnp.sum(jnp.where(key_ref[...] >= c, 1.0, 0.0), axis=-1, keepdims=True)
                + jnp.where(key_new >= c, 1.0, 0.0))

    tau = _kth_largest_key(count_ge, rows, topk)
    key = key_ref[...]
    bias_ref[...] = jnp.where(key >= tau, 0.0, NEG_INF)
    n_gt = (jnp.sum(jnp.where(key > tau, 1.0, 0.0), axis=-1, keepdims=True)
            + jnp.where(key_new > tau, 1.0, 0.0))
    need = jnp.float32(topk) - n_gt
    bias_new_ref[...] = jnp.broadcast_to(jnp.where(key_new >= tau, 0.0, NEG_INF), (rows, LANES))

    @pl.when(jnp.max(count_ge(tau)) > jnp.float32(topk))
    def _():
        n_eq_past = _tie_break_bias(key_ref, bias_ref, tau, need, past)
        keep_new = (key_new > tau) | ((key_new == tau) & (n_eq_past + 1.0 <= need))
        bias_new_ref[...] = jnp.broadcast_to(jnp.where(keep_new, 0.0, NEG_INF), (rows, LANES))


def _sample_select(scores, qi, kiwi, topk):
    rows, past = scores.shape
    return pl.pallas_call(
        functools.partial(_sample_select_kernel, topk=topk),
        out_shape=[jax.ShapeDtypeStruct((rows, past), F32), jax.ShapeDtypeStruct((rows, LANES), F32)],
        scratch_shapes=[pltpu.VMEM((rows, past), jnp.int32)],
        compiler_params=pltpu.CompilerParams(vmem_limit_bytes=VMEM_LIMIT),
        name="sample_select",
    )(scores, qi, kiwi)


def _sample_attn_kernel(pt_ref, aq_ref, knew_ref, vnew_ref, bias_ref, bias_new_ref, k_hbm, v_hbm, o_ref,
                        kbuf, vbuf, sem, *, n_pages):
    b = pl.program_id(0)
    nb = pl.num_programs(0)
    slot = b % 2

    def copies(bb, sl):
        dst_k = lambda p: kbuf.at[sl, :, :, pl.ds(p * PAGE_SIZE, PAGE_SIZE)]
        dst_v = lambda p: vbuf.at[sl, :, :, pl.ds(p * PAGE_SIZE, PAGE_SIZE)]
        return (_page_copies(pt_ref, bb, n_pages, k_hbm, dst_k, sem.at[0, sl])
                + _page_copies(pt_ref, bb, n_pages, v_hbm, dst_v, sem.at[1, sl]))

    @pl.when(b == 0)
    def _():
        for cp in copies(b, slot):
            cp.start()

    @pl.when(b + 1 < nb)
    def _():
        for cp in copies(b + 1, 1 - slot):
            cp.start()

    for cp in copies(b, slot):
        cp.wait()
    bias = bias_ref[0]
    bias_new = bias_new_ref[0][:, 0:1]
    for g in range(N_KV_HEADS):
        aq = aq_ref[0, g]
        logits = jnp.dot(aq, kbuf[slot, g].astype(BF16), preferred_element_type=F32) + bias
        knew = knew_ref[0, g].astype(BF16).astype(F32)
        l_new = jnp.sum(aq.astype(F32) * knew, axis=-1, keepdims=True) + bias_new
        m = jnp.maximum(jnp.max(logits, axis=-1, keepdims=True), l_new)
        p = jnp.exp(logits - m)
        p_new = jnp.exp(l_new - m)
        l = jnp.sum(p, axis=-1, keepdims=True) + p_new
        pv = lax.dot_general(p.astype(BF16), vbuf[slot, g].astype(BF16), (((1,), (1,)), ((), ())),
                             preferred_element_type=F32)
        pv = pv + p_new.astype(BF16).astype(F32) * vnew_ref[0, g].astype(BF16).astype(F32)
        o_ref[0, g] = pv * (1.0 / l)


def _sample_attention(pt_flat, aq, knew, vnew, bias, bias_new, k_t, v_t, n_samples, n_pages):
    past = n_pages * PAGE_SIZE
    blk4 = lambda b, pt: (b, 0, 0, 0)
    blk3 = lambda b, pt: (b, 0, 0)
    return pl.pallas_call(
        functools.partial(_sample_attn_kernel, n_pages=n_pages),
        grid_spec=pltpu.PrefetchScalarGridSpec(
            num_scalar_prefetch=1,
            grid=(n_samples,),
            in_specs=[
                pl.BlockSpec((1, N_KV_HEADS, 16, HEAD_DIM), blk4),
                pl.BlockSpec((1, N_KV_HEADS, 1, HEAD_DIM), blk4),
                pl.BlockSpec((1, N_KV_HEADS, 1, HEAD_DIM), blk4),
                pl.BlockSpec((1, 1, past), blk3),
                pl.BlockSpec((1, 1, LANES), blk3),
                pl.BlockSpec(memory_space=pl.ANY),
                pl.BlockSpec(memory_space=pl.ANY),
            ],
            out_specs=pl.BlockSpec((1, N_KV_HEADS, 16, HEAD_DIM), blk4),
            scratch_shapes=[
                pltpu.VMEM((2, N_KV_HEADS, HEAD_DIM, past), F32),
                pltpu.VMEM((2, N_KV_HEADS, HEAD_DIM, past), F32),
                pltpu.SemaphoreType.DMA((2, 2)),
            ],
        ),
        out_shape=jax.ShapeDtypeStruct((n_samples, N_KV_HEADS, 16, HEAD_DIM), F32),
        compiler_params=pltpu.CompilerParams(
            dimension_semantics=("arbitrary",), vmem_limit_bytes=VMEM_LIMIT),
        name="sample_attention",
    )(pt_flat, aq, knew, vnew, bias, bias_new, k_t, v_t)


def _out_ln_kernel(a_ref, b_ref, x_ref, w_ref, g_ref, beta_ref, o_ref):
    mix = (jnp.dot(a_ref[...], w_ref[0], preferred_element_type=F32)
           + jnp.dot(b_ref[...], w_ref[1], preferred_element_type=F32))
    o_ref[...] = _layernorm(ALPHA * x_ref[...] + mix, g_ref[...], beta_ref[...])


def _out_ln(a, b, x, w, g, beta, tm):
    n = x.shape[0]
    row = lambda i: (i, 0)
    const2 = lambda i: (0, 0)
    return pl.pallas_call(
        _out_ln_kernel,
        grid=(n // tm,),
        in_specs=[
            pl.BlockSpec((tm, A_WIDTH), row),
            pl.BlockSpec((tm, N_HEADS * HEAD_DIM), row),
            pl.BlockSpec((tm, D_MODEL), row),
            pl.BlockSpec((2, A_WIDTH, D_MODEL), lambda i: (0, 0, 0)),
            pl.BlockSpec((1, D_MODEL), const2),
            pl.BlockSpec((1, D_MODEL), const2),
        ],
        out_specs=pl.BlockSpec((tm, D_MODEL), row),
        out_shape=jax.ShapeDtypeStruct((n, D_MODEL), F32),
        compiler_params=pltpu.CompilerParams(
            dimension_semantics=("arbitrary",), vmem_limit_bytes=VMEM_LIMIT),
        name="out_ln",
    )(a, b, x, w, g, beta)


def _rope_tables(pos):
    half = HEAD_DIM // 2
    inv = ROPE_THETA ** (-jnp.arange(half, dtype=F32) / half)
    ang = pos.astype(F32)[:, None] * inv[None, :]
    c = jnp.cos(ang)
    s = jnp.sin(ang)
    cos64 = jnp.concatenate([c, c], axis=1)
    sin64 = jnp.concatenate([-s, s], axis=1)
    n = pos.shape[0]
    cos = jnp.concatenate([cos64, cos64], axis=1)
    sin = jnp.concatenate([sin64, sin64], axis=1)
    cosk = jnp.concatenate([cos64, jnp.full((n, LANES - IDX_DIM), IDX_W_SCALE, F32)], axis=1)
    sink = jnp.concatenate([sin64, jnp.zeros((n, LANES - IDX_DIM), F32)], axis=1)
    return cos, sin, cosk, sink


def _prep_ffn_weights(w_up, w_down):
    gate = w_up[:, :D_FF].reshape(D_MODEL, N_FF_CHUNKS, FF_CHUNK)
    up = w_up[:, D_FF:].reshape(D_MODEL, N_FF_CHUNKS, FF_CHUNK)
    wgu = jnp.concatenate([gate, up], axis=2).transpose(1, 0, 2).astype(BF16)
    wd = w_down.reshape(N_FF_CHUNKS, FF_CHUNK, D_MODEL).astype(BF16)
    return wgu, wd


def kernel(x_prompt, x_sample, cache_k, cache_v, cache_kidx, page_table, ln1_g, ln1_b, ffn1_w_up, ffn1_w_down, ln2_g, ln2_b, w_in, a_ln_g, a_ln_b, a_ws, a_bs, w_out, ln3_g, ln3_b, ffn2_w_up, ffn2_w_down):
    batch, seq, _ = x_prompt.shape
    n_samples = x_sample.shape[0]
    n_pages = page_table.shape[1]
    past = n_pages * PAGE_SIZE
    l = 0

    wgu1, wd1 = _prep_ffn_weights(ffn1_w_up[l], ffn1_w_down[l])
    wgu2, wd2 = _prep_ffn_weights(ffn2_w_up[l], ffn2_w_down[l])
    w_in_p = jnp.pad(w_in[l], ((0, 0), (0, IN_PAD - IN_WIDTH))).astype(BF16)
    w_out_p = w_out[l].reshape(2, A_WIDTH, D_MODEL).astype(BF16)
    causal = jnp.tril(jnp.ones((CHUNK, CHUNK), dtype=bool))
    ws_prompt = jnp.where(causal[None], a_ws[l], 0).astype(BF16)
    bias_prompt = jnp.repeat(a_bs[l].T, A_WIDTH // A_GROUPS, axis=1)
    eye = jnp.eye(CHUNK, dtype=F32)
    ws_sample = (a_ws[l][:, 0, 0][:, None, None] * eye[None]).astype(BF16)
    bias_sample = jnp.broadcast_to(bias_prompt[0:1], (CHUNK, A_WIDTH))

    pos_p = jnp.arange(seq, dtype=jnp.int32)
    pos_s = jnp.full((n_samples,), past, dtype=jnp.int32)
    tabs_p = _rope_tables(pos_p)
    tabs_s = _rope_tables(pos_s)

    xp = x_prompt.reshape(batch * seq, D_MODEL)
    x1p = _ffn_ln(xp, wgu1, wd1, ln1_g, ln1_b, tm=512)
    a_p, q_p, qi_p, kiwi_p, kt_p, vt_p, kit_p = _mixer_in(
        x1p, w_in_p, *tabs_p, a_ln_g, a_ln_b, ws_prompt, bias_prompt, batch, seq, tm=512, emit_av=False)
    b_p = _prompt_attention(q_p, qi_p, kiwi_p, kit_p, kt_p, vt_p, batch, seq)
    x2p = _out_ln(a_p, b_p, x1p, w_out_p, ln2_g, ln2_b, tm=512)
    yp = _ffn_ln(x2p, wgu2, wd2, ln3_g, ln3_b, tm=512)

    xs = x_sample.reshape(n_samples, D_MODEL)
    x1s = _ffn_ln(xs, wgu1, wd1, ln1_g, ln1_b, tm=n_samples)
    a_s, q_s, qi_s, kiwi_s, kt_s, vt_s, kit_s, av_s = _mixer_in(
        x1s, w_in_p, *tabs_s, a_ln_g, a_ln_b, ws_sample, bias_sample, 1, n_samples, tm=n_samples, emit_av=True)
    pt_flat = page_table.reshape(-1)
    kidx_t = jnp.transpose(cache_kidx[l], (0, 2, 1))
    k_t = jnp.transpose(cache_k[l], (0, 2, 3, 1))
    v_t = jnp.transpose(cache_v[l], (0, 2, 3, 1))
    a_idx = jnp.pad(qi_s.reshape(n_samples, N_IDX_HEADS, IDX_DIM), ((0, 0), (0, 16 - N_IDX_HEADS), (0, 0)))
    w_idx = jnp.pad(kiwi_s[:, IDX_DIM:IDX_DIM + N_IDX_HEADS], ((0, 0), (0, 16 - N_IDX_HEADS)))[..., None]
    scores = _sample_scores(pt_flat, a_idx, w_idx, kidx_t, n_samples, n_pages)
    topk = min(TOPK_MAX, (past + 1) // 4)
    bias_s, bias_new = _sample_select(scores.reshape(n_samples, past), qi_s, kiwi_s, topk)
    aq = jnp.pad(q_s.reshape(n_samples, N_KV_HEADS, GROUP_SIZE, HEAD_DIM),
                 ((0, 0), (0, 0), (0, 16 - GROUP_SIZE), (0, 0)))
    knew = jnp.transpose(kt_s[0], (2, 0, 1))[:, :, None, :]
    vnew = jnp.transpose(vt_s[0], (2, 0, 1))[:, :, None, :]
    o_s = _sample_attention(pt_flat, aq, knew, vnew, bias_s.reshape(n_samples, 1, past),
                            bias_new.reshape(n_samples, 1, LANES), k_t, v_t, n_samples, n_pages)
    b_s = o_s[:, :, :GROUP_SIZE, :].reshape(n_samples, N_HEADS * HEAD_DIM).astype(BF16)
    x2s = _out_ln(a_s, b_s, x1s, w_out_p, ln2_g, ln2_b, tm=n_samples)
    ys = _ffn_ln(x2s, wgu2, wd2, ln3_g, ln3_b, tm=n_samples)

    new_k_p = jnp.transpose(kt_p, (0, 3, 1, 2))[None]
    new_v_p = jnp.transpose(vt_p, (0, 3, 1, 2))[None]
    new_ki_p = jnp.transpose(kit_p, (0, 2, 1))[None]
    new_k_s = jnp.transpose(kt_s[0], (2, 0, 1))[None, :, None]
    new_v_s = jnp.transpose(vt_s[0], (2, 0, 1))[None, :, None]
    new_ki_s = jnp.transpose(kit_s[0], (1, 0))[None, :, None]
    return (yp.reshape(batch, seq, D_MODEL), ys.reshape(n_samples, 1, D_MODEL),
            new_k_p, new_v_p, new_ki_p, new_k_s, new_v_s, new_ki_s,
            av_s.reshape(1, n_samples, 1, A_WIDTH))
```

```python
import functools

import jax
import jax.numpy as jnp
from jax import lax
from jax.experimental import pallas as pl
from jax.experimental.pallas import tpu as pltpu

F32 = jnp.float32
BF16 = jnp.bfloat16

D_MODEL = 1024
D_FF = 2816
A_WIDTH = 512
A_GROUPS = 4
CHUNK = 128
HEAD_DIM = 64
N_HEADS = 8
N_KV_HEADS = 2
GROUP_SIZE = N_HEADS // N_KV_HEADS
N_IDX_HEADS = 4
IDX_DIM = 64
IDX_W_SCALE = (N_IDX_HEADS * IDX_DIM) ** -0.5
TOPK_MAX = 256
PAGE_SIZE = 128
ROPE_THETA = 10000.0
LN_EPS = 1e-5
DEPTH = 1
ALPHA = (2.0 * DEPTH) ** 0.25
ATTN_SCALE = HEAD_DIM ** -0.5

IN_SPLITS = (512, 512, 512, 128, 128, 256, 64, 4)
IN_WIDTH = sum(IN_SPLITS)
LANES = 128
IN_PAD = -(-IN_WIDTH // LANES) * LANES
COL_Q = 1024
COL_K = 1536
COL_V = 1664
COL_QI = 1792
COL_KIWI = 2048

FF_CHUNK = 256
N_FF_CHUNKS = D_FF // FF_CHUNK
VMEM_LIMIT = 56 * 1024 * 1024
INT_MIN = -2 ** 31
NEG_INF = float("-inf")


def _layernorm(y, g, b):
    mu = jnp.mean(y, axis=-1, keepdims=True)
    d = y - mu
    var = jnp.mean(d * d, axis=-1, keepdims=True)
    return d * lax.rsqrt(var + LN_EPS) * g + b


def _gelu_tanh(x):
    return 0.5 * x * (1.0 + jnp.tanh(0.7978845608028654 * (x + 0.044715 * (x * x * x))))


def _ffn_ln_kernel(x_ref, wgu_ref, wd_ref, g_ref, b_ref, o_ref):
    x = x_ref[...]
    xb = x.astype(BF16)
    acc = jnp.zeros(x.shape, F32)
    for c in range(N_FF_CHUNKS):
        gu = jnp.dot(xb, wgu_ref[c], preferred_element_type=F32)
        gate = gu[:, :FF_CHUNK]
        up = gu[:, FF_CHUNK:]
        h = (gate * (1.0 / (1.0 + jnp.exp(-gate))) * up).astype(BF16)
        acc = acc + jnp.dot(h, wd_ref[c], preferred_element_type=F32)
    o_ref[...] = _layernorm(ALPHA * x + 0.5 * acc, g_ref[...], b_ref[...])


def _ffn_ln(x, wgu, wd, g, b, tm):
    n = x.shape[0]
    const3 = lambda i: (0, 0, 0)
    const2 = lambda i: (0, 0)
    return pl.pallas_call(
        _ffn_ln_kernel,
        grid=(n // tm,),
        in_specs=[
            pl.BlockSpec((tm, D_MODEL), lambda i: (i, 0)),
            pl.BlockSpec((N_FF_CHUNKS, D_MODEL, 2 * FF_CHUNK), const3),
            pl.BlockSpec((N_FF_CHUNKS, FF_CHUNK, D_MODEL), const3),
            pl.BlockSpec((1, D_MODEL), const2),
            pl.BlockSpec((1, D_MODEL), const2),
        ],
        out_specs=pl.BlockSpec((tm, D_MODEL), lambda i: (i, 0)),
        out_shape=jax.ShapeDtypeStruct((n, D_MODEL), F32),
        compiler_params=pltpu.CompilerParams(
            dimension_semantics=("arbitrary",), vmem_limit_bytes=VMEM_LIMIT),
        name="ffn_ln",
    )(x, wgu, wd, g, b)


def _mixer_in_kernel(x_ref, w_ref, cos_ref, sin_ref, cosk_ref, sink_ref, ag_ref, ab_ref, ws_ref,
                     bias_ref, aout_ref, kt_ref, vt_ref, kit_ref, *mode_refs, sample):
    tm = x_ref.shape[0]
    xb = x_ref[...].astype(BF16)

    def proj(lo, hi):
        return jnp.dot(xb, w_ref[:, lo:hi], preferred_element_type=F32)

    u = _gelu_tanh(proj(0, A_WIDTH))
    vn = _layernorm(_gelu_tanh(proj(A_WIDTH, 2 * A_WIDTH)), ag_ref[...], ab_ref[...])
    vnb = vn.astype(BF16)
    for c in range(tm // CHUNK):
        rows = slice(c * CHUNK, (c + 1) * CHUNK)
        for g in range(A_GROUPS):
            cols = slice(g * LANES, (g + 1) * LANES)
            mixed = jnp.dot(ws_ref[g], vnb[rows, cols], preferred_element_type=F32) + bias_ref[:, cols]
            aout_ref[rows, cols] = (u[rows, cols] * mixed).astype(BF16)

    lane = lax.broadcasted_iota(jnp.int32, (tm, LANES), 1)
    first_half = (lane & (HEAD_DIM - 1)) < (HEAD_DIM // 2)

    def rope(x, cos, sin):
        partner = jnp.where(first_half, pltpu.roll(x, LANES - HEAD_DIM // 2, 1),
                            pltpu.roll(x, HEAD_DIM // 2, 1))
        return x * cos + partner * sin

    cos = cos_ref[...]
    sin = sin_ref[...]
    qall = proj(COL_Q, COL_K)
    rest = proj(COL_K, IN_PAD)
    k = rope(rest[:, 0:128], cos, sin)
    v = rest[:, 128:256]
    kiwi = rope(rest[:, 512:640], cosk_ref[...], sink_ref[...])
    k_t = k.T
    v_t = v.T
    kiwi_t = kiwi.T
    kt_ref[0] = k_t.reshape(N_KV_HEADS, HEAD_DIM, tm)
    vt_ref[0] = v_t.reshape(N_KV_HEADS, HEAD_DIM, tm)
    kit_ref[0] = kiwi_t[0:IDX_DIM]
    if sample:
        q_ref, qi_ref, kiwi_ref, av_ref = mode_refs
        av_ref[...] = vn
        kiwi_ref[...] = kiwi
        for j in range(4):
            cols = slice(j * LANES, (j + 1) * LANES)
            q_ref[:, cols] = (rope(qall[:, cols], cos, sin) * ATTN_SCALE).astype(BF16)
        for j in range(2):
            cols = slice(j * LANES, (j + 1) * LANES)
            qi_ref[:, cols] = rope(rest[:, 256 + j * LANES:256 + (j + 1) * LANES], cos, sin).astype(BF16)
    else:
        qt_ref, qit_ref, wit_ref, kb_ref, kib_ref, vtb_ref = mode_refs
        for j in range(4):
            cols = slice(j * LANES, (j + 1) * LANES)
            qt_ref[0, cols, :] = (rope(qall[:, cols], cos, sin) * ATTN_SCALE).T.astype(BF16)
        for j in range(2):
            cols = slice(j * LANES, (j + 1) * LANES)
            qit_ref[0, cols, :] = rope(rest[:, 256 + j * LANES:256 + (j + 1) * LANES], cos, sin).T.astype(BF16)
        wit_ref[0] = kiwi_t[IDX_DIM:IDX_DIM + 8]
        kb_ref[...] = k.astype(BF16)
        kib_ref[...] = kiwi.astype(BF16)
        vtb_ref[0] = v_t.astype(BF16)


def _mixer_in(x, w_in, cos, sin, cosk, sink, ag, ab, ws, bias, batch, seq, tm, sample):
    n = batch * seq
    nj = seq // tm
    row = lambda b, j: (b * nj + j, 0)
    tab = lambda b, j: (j, 0)
    const2 = lambda b, j: (0, 0)
    const3 = lambda b, j: (0, 0, 0)
    fmajor = lambda b, j: (b, 0, j)
    n_q = N_HEADS * HEAD_DIM
    n_qi = N_IDX_HEADS * IDX_DIM
    out_shape = [
        jax.ShapeDtypeStruct((n, A_WIDTH), BF16),
        jax.ShapeDtypeStruct((batch, N_KV_HEADS, HEAD_DIM, seq), F32),
        jax.ShapeDtypeStruct((batch, N_KV_HEADS, HEAD_DIM, seq), F32),
        jax.ShapeDtypeStruct((batch, IDX_DIM, seq), F32),
    ]
    out_specs = [
        pl.BlockSpec((tm, A_WIDTH), row),
        pl.BlockSpec((1, N_KV_HEADS, HEAD_DIM, tm), lambda b, j: (b, 0, 0, j)),
        pl.BlockSpec((1, N_KV_HEADS, HEAD_DIM, tm), lambda b, j: (b, 0, 0, j)),
        pl.BlockSpec((1, IDX_DIM, tm), fmajor),
    ]
    if sample:
        out_shape += [
            jax.ShapeDtypeStruct((n, n_q), BF16),
            jax.ShapeDtypeStruct((n, n_qi), BF16),
            jax.ShapeDtypeStruct((n, LANES), F32),
            jax.ShapeDtypeStruct((n, A_WIDTH), F32),
        ]
        out_specs += [
            pl.BlockSpec((tm, n_q), row),
            pl.BlockSpec((tm, n_qi), row),
            pl.BlockSpec((tm, LANES), row),
            pl.BlockSpec((tm, A_WIDTH), row),
        ]
    else:
        out_shape += [
            jax.ShapeDtypeStruct((batch, n_q, seq), BF16),
            jax.ShapeDtypeStruct((batch, n_qi, seq), BF16),
            jax.ShapeDtypeStruct((batch, 8, seq), F32),
            jax.ShapeDtypeStruct((n, LANES), BF16),
            jax.ShapeDtypeStruct((n, LANES), BF16),
            jax.ShapeDtypeStruct((batch, LANES, seq), BF16),
        ]
        out_specs += [
            pl.BlockSpec((1, n_q, tm), fmajor),
            pl.BlockSpec((1, n_qi, tm), fmajor),
            pl.BlockSpec((1, 8, tm), fmajor),
            pl.BlockSpec((tm, LANES), row),
            pl.BlockSpec((tm, LANES), row),
            pl.BlockSpec((1, LANES, tm), fmajor),
        ]
    return pl.pallas_call(
        functools.partial(_mixer_in_kernel, sample=sample),
        grid=(batch, nj),
        in_specs=[
            pl.BlockSpec((tm, D_MODEL), row),
            pl.BlockSpec((D_MODEL, IN_PAD), const2),
            pl.BlockSpec((tm, LANES), tab),
            pl.BlockSpec((tm, LANES), tab),
            pl.BlockSpec((tm, LANES), tab),
            pl.BlockSpec((tm, LANES), tab),
            pl.BlockSpec((1, A_WIDTH), const2),
            pl.BlockSpec((1, A_WIDTH), const2),
            pl.BlockSpec((A_GROUPS, CHUNK, CHUNK), const3),
            pl.BlockSpec((CHUNK, A_WIDTH), const2),
        ],
        out_specs=out_specs,
        out_shape=out_shape,
        compiler_params=pltpu.CompilerParams(
            dimension_semantics=("arbitrary", "arbitrary"), vmem_limit_bytes=VMEM_LIMIT),
        name="mixer_in",
    )(x, w_in, cos, sin, cosk, sink, ag, ab, ws, bias)


def _sortable_key(s):
    s = jnp.where(s == 0.0, 0.0, s)
    bits = pltpu.bitcast(s, jnp.int32)
    return jnp.where(bits < 0, bits ^ jnp.int32(0x7FFFFFFF), bits)


def _kth_largest_key(count_ge, rows, k):
    kf = jnp.float32(k)
    zero = jnp.zeros((rows, 1), jnp.int32)
    cur = jnp.where(count_ge(zero) >= kf, zero, jnp.int32(INT_MIN))

    def body(it, cur):
        cand = cur | jnp.left_shift(jnp.int32(1), jnp.int32(30) - it)
        return jnp.where(count_ge(cand) >= kf, cand, cur)

    return lax.fori_loop(0, 31, body, cur)


def _upper_tri_ones():
    r = lax.broadcasted_iota(jnp.int32, (LANES, LANES), 0)
    c = lax.broadcasted_iota(jnp.int32, (LANES, LANES), 1)
    return jnp.where(r <= c, 1.0, 0.0).astype(BF16)


def _tie_break_bias(key_ref, bias_ref, tau, need, n_cols):
    tri = _upper_tri_ones()
    rows = tau.shape[0]
    run = jnp.zeros((rows, 1), F32)
    for c in range(n_cols // LANES):
        cols = slice(c * LANES, (c + 1) * LANES)
        kc = key_ref[:, cols]
        eq = jnp.where(kc == tau, 1.0, 0.0)
        rank = run + jnp.dot(eq.astype(BF16), tri, preferred_element_type=F32)
        keep = (kc > tau) | ((kc == tau) & (rank <= need))
        bias_ref[:, cols] = jnp.where(keep, 0.0, NEG_INF)
        run = run + jnp.sum(eq, axis=-1, keepdims=True)
    return run


Q_BLOCK = 256
INT16_MIN = -2 ** 15
PACK16 = 16
N_COUNT_ACCS = 8


def _count_rows_16(mask_of_group, n_rows, lanes):
    one = jnp.ones((PACK16, lanes), jnp.int16)
    zero = jnp.zeros((PACK16, lanes), jnp.int16)
    accs = [zero] * min(N_COUNT_ACCS, n_rows // PACK16)
    for r in range(n_rows // PACK16):
        accs[r % len(accs)] = accs[r % len(accs)] + jnp.where(mask_of_group(r), one, zero)
    parts = accs
    while len(parts) > 1:
        parts = [parts[j] + parts[j + 1] for j in range(0, len(parts), 2)]
    return jnp.sum(parts[0].astype(jnp.int32), axis=0, keepdims=True)


def _bcast16(x, lanes):
    return jnp.broadcast_to(x, (PACK16, lanes)).astype(jnp.int16)


def _kth_largest_16(ref, n_rows, k):
    lanes = ref.shape[1]

    def body(it, cur):
        cand = cur + jnp.left_shift(jnp.int32(1), jnp.int32(15) - it)
        c16 = _bcast16(cand, lanes)
        cnt = _count_rows_16(lambda r: ref[r * PACK16:(r + 1) * PACK16, :] >= c16, n_rows, lanes)
        return jnp.where(cnt >= k, cand, cur)

    return lax.fori_loop(0, 16, body, jnp.full((1, lanes), INT16_MIN, jnp.int32))


def _lower_tri_ones():
    r = lax.broadcasted_iota(jnp.int32, (LANES, LANES), 0)
    c = lax.broadcasted_iota(jnp.int32, (LANES, LANES), 1)
    return jnp.where(c <= r, 1.0, 0.0).astype(BF16)


def _tie_break_bias_rows(key_ref, bias_ref, tau, need, n_rows):
    tri = _lower_tri_ones()
    run = jnp.zeros(tau.shape, F32)
    for c in range(n_rows // LANES):
        rows = slice(c * LANES, (c + 1) * LANES)
        kc = key_ref[rows, :]
        eq = jnp.where(kc == tau, 1.0, 0.0)
        rank = run + jnp.dot(tri, eq.astype(BF16), preferred_element_type=F32)
        keep = (kc > tau) | ((kc == tau) & (rank <= need))
        bias_ref[rows, :] = jnp.where(keep, 0.0, NEG_INF)
        run = run + jnp.sum(eq, axis=0, keepdims=True)


def _prompt_attn_block(n_keys, search, i, qt_ref, qit_ref, wit_ref, kib_ref, kb_ref, vtb_ref, o_ref,
                       key_ref, hi_ref, lo_ref, bias_ref):
    tq = Q_BLOCK
    zeros64 = jnp.zeros((HEAD_DIM, tq), BF16)
    qi_rhs = jnp.concatenate(
        [jnp.concatenate([qit_ref[0, h * IDX_DIM:(h + 1) * IDX_DIM, :], zeros64], axis=0)
         for h in range(N_IDX_HEADS)], axis=1)
    dots = jnp.dot(kib_ref[0:n_keys, :], qi_rhs, preferred_element_type=F32)
    wi = wit_ref[0]
    s = jnp.zeros((n_keys, tq), F32)
    for h in range(N_IDX_HEADS):
        s = s + wi[h:h + 1, :] * jnp.maximum(dots[:, h * tq:(h + 1) * tq], 0.0)
    key_pos = lax.broadcasted_iota(jnp.int32, (n_keys, tq), 0)
    q_pos = i * tq + lax.broadcasted_iota(jnp.int32, (n_keys, tq), 1)
    visible = key_pos <= q_pos

    if not search:
        bias_ref[0:n_keys, :] = jnp.where(visible, 0.0, NEG_INF)
    else:
        key = _sortable_key(jnp.where(visible, s, NEG_INF))
        key_ref[0:n_keys, :] = key
        hi_ref[0:n_keys, :] = (key >> 16).astype(jnp.int16)
        lo_ref[0:n_keys, :] = ((key & 0xFFFF) + INT16_MIN).astype(jnp.int16)
        tau_hi = _kth_largest_16(hi_ref, n_keys, jnp.int32(TOPK_MAX))
        t16 = _bcast16(tau_hi, tq)
        n_above = _count_rows_16(lambda r: hi_ref[r * PACK16:(r + 1) * PACK16, :] > t16, n_keys, tq)
        min16 = jnp.full((PACK16, tq), INT16_MIN, jnp.int16)
        for r in range(n_keys // PACK16):
            rows = slice(r * PACK16, (r + 1) * PACK16)
            lo_ref[rows, :] = jnp.where(hi_ref[rows, :] == t16, lo_ref[rows, :], min16)
        tau_lo = _kth_largest_16(lo_ref, n_keys, jnp.int32(TOPK_MAX) - n_above)
        tau = jnp.left_shift(tau_hi, 16) | (tau_lo - INT16_MIN)
        key = key_ref[0:n_keys, :]
        bias_ref[0:n_keys, :] = jnp.where(key >= tau, 0.0, NEG_INF)
        n_gt = jnp.sum(jnp.where(key > tau, 1.0, 0.0), axis=0, keepdims=True)
        n_ge = jnp.sum(jnp.where(key >= tau, 1.0, 0.0), axis=0, keepdims=True)

        @pl.when(jnp.max(n_ge) > jnp.float32(TOPK_MAX))
        def _():
            _tie_break_bias_rows(key_ref, bias_ref, tau, jnp.float32(TOPK_MAX) - n_gt, n_keys)

    bias = bias_ref[0:n_keys, :]
    outs = []
    for g in range(N_KV_HEADS):
        blocks = []
        for r in range(GROUP_SIZE):
            h = g * GROUP_SIZE + r
            qh = qt_ref[0, h * HEAD_DIM:(h + 1) * HEAD_DIM, :]
            blocks.append(jnp.concatenate([qh, zeros64] if g == 0 else [zeros64, qh], axis=0))
        logits = jnp.dot(kb_ref[0:n_keys, :], jnp.concatenate(blocks, axis=1),
                         preferred_element_type=F32)
        vt = vtb_ref[0, g * HEAD_DIM:(g + 1) * HEAD_DIM, 0:n_keys]
        for r in range(GROUP_SIZE):
            lg = logits[:, r * tq:(r + 1) * tq] + bias
            m = jnp.max(lg, axis=0, keepdims=True)
            p = jnp.exp(lg - m)
            l = jnp.sum(p, axis=0, keepdims=True)
            pv = jnp.dot(vt, p.astype(BF16), preferred_element_type=F32)
            outs.append(pv * (1.0 / l))
    o_ref[...] = jnp.concatenate(outs, axis=0).T.astype(BF16)


def _prompt_attn_kernel(qt_ref, qit_ref, wit_ref, kib_ref, kb_ref, vtb_ref, o_ref,
                        key_ref, hi_ref, lo_ref, bias_ref):
    i = pl.program_id(1)
    for c in range(key_ref.shape[0] // Q_BLOCK):
        @pl.when(i == c)
        def _(c=c):
            _prompt_attn_block((c + 1) * Q_BLOCK, c >= 1, i, qt_ref, qit_ref, wit_ref, kib_ref, kb_ref,
                               vtb_ref, o_ref, key_ref, hi_ref, lo_ref, bias_ref)


def _prompt_attention(qt, qit, wit, kib, kb, vtb, batch, seq):
    nb = seq // Q_BLOCK
    n_q = N_HEADS * HEAD_DIM
    n_qi = N_IDX_HEADS * IDX_DIM
    qblk = lambda b, i: (b, 0, i)
    return pl.pallas_call(
        _prompt_attn_kernel,
        grid=(batch, nb),
        in_specs=[
            pl.BlockSpec((1, n_q, Q_BLOCK), qblk),
            pl.BlockSpec((1, n_qi, Q_BLOCK), qblk),
            pl.BlockSpec((1, 8, Q_BLOCK), qblk),
            pl.BlockSpec((seq, LANES), lambda b, i: (b, 0)),
            pl.BlockSpec((seq, LANES), lambda b, i: (b, 0)),
            pl.BlockSpec((1, LANES, seq), lambda b, i: (b, 0, 0)),
        ],
        out_specs=pl.BlockSpec((Q_BLOCK, n_q), lambda b, i: (b * nb + i, 0)),
        out_shape=jax.ShapeDtypeStruct((batch * seq, n_q), BF16),
        scratch_shapes=[pltpu.VMEM((seq, Q_BLOCK), jnp.int32), pltpu.VMEM((seq, Q_BLOCK), jnp.int16),
                        pltpu.VMEM((seq, Q_BLOCK), jnp.int16), pltpu.VMEM((seq, Q_BLOCK), F32)],
        compiler_params=pltpu.CompilerParams(
            dimension_semantics=("arbitrary", "arbitrary"), vmem_limit_bytes=VMEM_LIMIT),
        name="prompt_attention",
    )(qt, qit, wit, kib, kb, vtb)


def _page_copies(pt_ref, b, n_pages, src_hbm, dst_for_page, sem):
    return [pltpu.make_async_copy(src_hbm.at[pt_ref[b * n_pages + p]], dst_for_page(p), sem)
            for p in range(n_pages)]


def _sample_scores_kernel(pt_ref, a_ref, w_ref, kidx_hbm, o_ref, buf, sem, *, n_pages):
    b = pl.program_id(0)
    nb = pl.num_programs(0)
    slot = b % 2

    def copies(bb, sl):
        return _page_copies(pt_ref, bb, n_pages, kidx_hbm,
                            lambda p: buf.at[sl, :, pl.ds(p * PAGE_SIZE, PAGE_SIZE)], sem.at[sl])

    @pl.when(b == 0)
    def _():
        for cp in copies(b, slot):
            cp.start()

    @pl.when(b + 1 < nb)
    def _():
        for cp in copies(b + 1, 1 - slot):
            cp.start()

    for cp in copies(b, slot):
        cp.wait()
    dots = jnp.dot(a_ref[0], buf[slot].astype(BF16), preferred_element_type=F32)
    o_ref[0] = jnp.sum(jnp.maximum(dots, 0.0) * w_ref[0], axis=0, keepdims=True)


def _sample_scores(pt_flat, a, w, kidx_t, n_samples, n_pages):
    past = n_pages * PAGE_SIZE
    return pl.pallas_call(
        functools.partial(_sample_scores_kernel, n_pages=n_pages),
        grid_spec=pltpu.PrefetchScalarGridSpec(
            num_scalar_prefetch=1,
            grid=(n_samples,),
            in_specs=[
                pl.BlockSpec((1, 16, IDX_DIM), lambda b, pt: (b, 0, 0)),
                pl.BlockSpec((1, 16, 1), lambda b, pt: (b, 0, 0)),
                pl.BlockSpec(memory_space=pl.ANY),
            ],
            out_specs=pl.BlockSpec((1, 1, past), lambda b, pt: (b, 0, 0)),
            scratch_shapes=[pltpu.VMEM((2, IDX_DIM, past), F32), pltpu.SemaphoreType.DMA((2,))],
        ),
        out_shape=jax.ShapeDtypeStruct((n_samples, 1, past), F32),
        compiler_params=pltpu.CompilerParams(
            dimension_semantics=("arbitrary",), vmem_limit_bytes=VMEM_LIMIT),
        name="sample_scores",
    )(pt_flat, a, w, kidx_t)


def _sample_select_kernel(s_ref, qi_ref, kiwi_ref, bias_ref, bias_new_ref, key_ref, *, topk):
    rows, past = s_ref.shape
    lane = lax.broadcasted_iota(jnp.int32, (rows, LANES), 1)
    lo = lane < IDX_DIM
    kiwi = kiwi_ref[...]
    kib = kiwi.astype(BF16).astype(F32)
    kidup = jnp.where(lo, kib, pltpu.roll(kib, IDX_DIM, 1))
    s_new = jnp.zeros((rows, 1), F32)
    for j in range(2):
        t = qi_ref[:, j * LANES:(j + 1) * LANES].astype(F32) * kidup
        for half in range(2):
            d = jnp.sum(jnp.where(lo if half == 0 else jnp.logical_not(lo), t, 0.0), axis=-1, keepdims=True)
            h = 2 * j + half
            s_new = s_new + kiwi[:, IDX_DIM + h:IDX_DIM + h + 1] * jnp.maximum(d, 0.0)
    key_new = _sortable_key(s_new)
    key_ref[...] = _sortable_key(s_ref[...])

    def count_ge(c):
        return (jnp.sum(jnp.where(key_ref[...] >= c, 1.0, 0.0), axis=-1, keepdims=True)
                + jnp.where(key_new >= c, 1.0, 0.0))

    tau = _kth_largest_key(count_ge, rows, topk)
    key = key_ref[...]
    bias_ref[...] = jnp.where(key >= tau, 0.0, NEG_INF)
    n_gt = (jnp.sum(jnp.where(key > tau, 1.0, 0.0), axis=-1, keepdims=True)
            + jnp.where(key_new > tau, 1.0, 0.0))
    need = jnp.float32(topk) - n_gt
    bias_new_ref[...] = jnp.broadcast_to(jnp.where(key_new >= tau, 0.0, NEG_INF), (rows, LANES))

    @pl.when(jnp.max(count_ge(tau)) > jnp.float32(topk))
    def _():
        n_eq_past = _tie_break_bias(key_ref, bias_ref, tau, need, past)
        keep_new = (key_new > tau) | ((key_new == tau) & (n_eq_past + 1.0 <= need))
        bias_new_ref[...] = jnp.broadcast_to(jnp.where(keep_new, 0.0, NEG_INF), (rows, LANES))


def _sample_select(scores, qi, kiwi, topk):
    rows, past = scores.shape
    return pl.pallas_call(
        functools.partial(_sample_select_kernel, topk=topk),
        out_shape=[jax.ShapeDtypeStruct((rows, past), F32), jax.ShapeDtypeStruct((rows, LANES), F32)],
        scratch_shapes=[pltpu.VMEM((rows, past), jnp.int32)],
        compiler_params=pltpu.CompilerParams(vmem_limit_bytes=VMEM_LIMIT),
        name="sample_select",
    )(scores, qi, kiwi)


def _sample_attn_kernel(pt_ref, aq_ref, knew_ref, vnew_ref, bias_ref, bias_new_ref, k_hbm, v_hbm, o_ref,
                        kbuf, vbuf, sem, *, n_pages):
    b = pl.program_id(0)
    nb = pl.num_programs(0)
    slot = b % 2

    def copies(bb, sl):
        dst_k = lambda p: kbuf.at[sl, :, :, pl.ds(p * PAGE_SIZE, PAGE_SIZE)]
        dst_v = lambda p: vbuf.at[sl, :, :, pl.ds(p * PAGE_SIZE, PAGE_SIZE)]
        return (_page_copies(pt_ref, bb, n_pages, k_hbm, dst_k, sem.at[0, sl])
                + _page_copies(pt_ref, bb, n_pages, v_hbm, dst_v, sem.at[1, sl]))

    @pl.when(b == 0)
    def _():
        for cp in copies(b, slot):
            cp.start()

    @pl.when(b + 1 < nb)
    def _():
        for cp in copies(b + 1, 1 - slot):
            cp.start()

    for cp in copies(b, slot):
        cp.wait()
    bias = bias_ref[0]
    bias_new = bias_new_ref[0][:, 0:1]
    for g in range(N_KV_HEADS):
        aq = aq_ref[0, g]
        logits = jnp.dot(aq, kbuf[slot, g].astype(BF16), preferred_element_type=F32) + bias
        knew = knew_ref[0, g].astype(BF16).astype(F32)
        l_new = jnp.sum(aq.astype(F32) * knew, axis=-1, keepdims=True) + bias_new
        m = jnp.maximum(jnp.max(logits, axis=-1, keepdims=True), l_new)
        p = jnp.exp(logits - m)
        p_new = jnp.exp(l_new - m)
        l = jnp.sum(p, axis=-1, keepdims=True) + p_new
        pv = lax.dot_general(p.astype(BF16), vbuf[slot, g].astype(BF16), (((1,), (1,)), ((), ())),
                             preferred_element_type=F32)
        pv = pv + p_new.astype(BF16).astype(F32) * vnew_ref[0, g].astype(BF16).astype(F32)
        o_ref[0, g] = pv * (1.0 / l)


def _sample_attention(pt_flat, aq, knew, vnew, bias, bias_new, k_t, v_t, n_samples, n_pages):
    past = n_pages * PAGE_SIZE
    blk4 = lambda b, pt: (b, 0, 0, 0)
    blk3 = lambda b, pt: (b, 0, 0)
    return pl.pallas_call(
        functools.partial(_sample_attn_kernel, n_pages=n_pages),
        grid_spec=pltpu.PrefetchScalarGridSpec(
            num_scalar_prefetch=1,
            grid=(n_samples,),
            in_specs=[
                pl.BlockSpec((1, N_KV_HEADS, 16, HEAD_DIM), blk4),
                pl.BlockSpec((1, N_KV_HEADS, 1, HEAD_DIM), blk4),
                pl.BlockSpec((1, N_KV_HEADS, 1, HEAD_DIM), blk4),
                pl.BlockSpec((1, 1, past), blk3),
                pl.BlockSpec((1, 1, LANES), blk3),
                pl.BlockSpec(memory_space=pl.ANY),
                pl.BlockSpec(memory_space=pl.ANY),
            ],
            out_specs=pl.BlockSpec((1, N_KV_HEADS, 16, HEAD_DIM), blk4),
            scratch_shapes=[
                pltpu.VMEM((2, N_KV_HEADS, HEAD_DIM, past), F32),
                pltpu.VMEM((2, N_KV_HEADS, HEAD_DIM, past), F32),
                pltpu.SemaphoreType.DMA((2, 2)),
            ],
        ),
        out_shape=jax.ShapeDtypeStruct((n_samples, N_KV_HEADS, 16, HEAD_DIM), F32),
        compiler_params=pltpu.CompilerParams(
            dimension_semantics=("arbitrary",), vmem_limit_bytes=VMEM_LIMIT),
        name="sample_attention",
    )(pt_flat, aq, knew, vnew, bias, bias_new, k_t, v_t)


def _out_ln_kernel(a_ref, b_ref, x_ref, w_ref, g_ref, beta_ref, o_ref):
    mix = (jnp.dot(a_ref[...], w_ref[0], preferred_element_type=F32)
           + jnp.dot(b_ref[...], w_ref[1], preferred_element_type=F32))
    o_ref[...] = _layernorm(ALPHA * x_ref[...] + mix, g_ref[...], beta_ref[...])


def _out_ln(a, b, x, w, g, beta, tm):
    n = x.shape[0]
    row = lambda i: (i, 0)
    const2 = lambda i: (0, 0)
    return pl.pallas_call(
        _out_ln_kernel,
        grid=(n // tm,),
        in_specs=[
            pl.BlockSpec((tm, A_WIDTH), row),
            pl.BlockSpec((tm, N_HEADS * HEAD_DIM), row),
            pl.BlockSpec((tm, D_MODEL), row),
            pl.BlockSpec((2, A_WIDTH, D_MODEL), lambda i: (0, 0, 0)),
            pl.BlockSpec((1, D_MODEL), const2),
            pl.BlockSpec((1, D_MODEL), const2),
        ],
        out_specs=pl.BlockSpec((tm, D_MODEL), row),
        out_shape=jax.ShapeDtypeStruct((n, D_MODEL), F32),
        compiler_params=pltpu.CompilerParams(
            dimension_semantics=("arbitrary",), vmem_limit_bytes=VMEM_LIMIT),
        name="out_ln",
    )(a, b, x, w, g, beta)


def _rope_tables(pos):
    half = HEAD_DIM // 2
    inv = ROPE_THETA ** (-jnp.arange(half, dtype=F32) / half)
    ang = pos.astype(F32)[:, None] * inv[None, :]
    c = jnp.cos(ang)
    s = jnp.sin(ang)
    cos64 = jnp.concatenate([c, c], axis=1)
    sin64 = jnp.concatenate([-s, s], axis=1)
    n = pos.shape[0]
    cos = jnp.concatenate([cos64, cos64], axis=1)
    sin = jnp.concatenate([sin64, sin64], axis=1)
    cosk = jnp.concatenate([cos64, jnp.full((n, LANES - IDX_DIM), IDX_W_SCALE, F32)], axis=1)
    sink = jnp.concatenate([sin64, jnp.zeros((n, LANES - IDX_DIM), F32)], axis=1)
    return cos, sin, cosk, sink


def _prep_ffn_weights(w_up, w_down):
    gate = w_up[:, :D_FF].reshape(D_MODEL, N_FF_CHUNKS, FF_CHUNK)
    up = w_up[:, D_FF:].reshape(D_MODEL, N_FF_CHUNKS, FF_CHUNK)
    wgu = jnp.concatenate([gate, up], axis=2).transpose(1, 0, 2).astype(BF16)
    wd = w_down.reshape(N_FF_CHUNKS, FF_CHUNK, D_MODEL).astype(BF16)
    return wgu, wd


def kernel(x_prompt, x_sample, cache_k, cache_v, cache_kidx, page_table, ln1_g, ln1_b, ffn1_w_up, ffn1_w_down, ln2_g, ln2_b, w_in, a_ln_g, a_ln_b, a_ws, a_bs, w_out, ln3_g, ln3_b, ffn2_w_up, ffn2_w_down):
    batch, seq, _ = x_prompt.shape
    n_samples = x_sample.shape[0]
    n_pages = page_table.shape[1]
    past = n_pages * PAGE_SIZE
    l = 0

    wgu1, wd1 = _prep_ffn_weights(ffn1_w_up[l], ffn1_w_down[l])
    wgu2, wd2 = _prep_ffn_weights(ffn2_w_up[l], ffn2_w_down[l])
    w_in_p = jnp.pad(w_in[l], ((0, 0), (0, IN_PAD - IN_WIDTH))).astype(BF16)
    w_out_p = w_out[l].reshape(2, A_WIDTH, D_MODEL).astype(BF16)
    causal = jnp.tril(jnp.ones((CHUNK, CHUNK), dtype=bool))
    ws_prompt = jnp.where(causal[None], a_ws[l], 0).astype(BF16)
    bias_prompt = jnp.repeat(a_bs[l].T, A_WIDTH // A_GROUPS, axis=1)
    eye = jnp.eye(CHUNK, dtype=F32)
    ws_sample = (a_ws[l][:, 0, 0][:, None, None] * eye[None]).astype(BF16)
    bias_sample = jnp.broadcast_to(bias_prompt[0:1], (CHUNK, A_WIDTH))

    pos_p = jnp.arange(seq, dtype=jnp.int32)
    pos_s = jnp.full((n_samples,), past, dtype=jnp.int32)
    tabs_p = _rope_tables(pos_p)
    tabs_s = _rope_tables(pos_s)

    xp = x_prompt.reshape(batch * seq, D_MODEL)
    x1p = _ffn_ln(xp, wgu1, wd1, ln1_g, ln1_b, tm=512)
    a_p, kt_p, vt_p, kit_p, qt_p, qit_p, wit_p, kb_p, kib_p, vtb_p = _mixer_in(
        x1p, w_in_p, *tabs_p, a_ln_g, a_ln_b, ws_prompt, bias_prompt, batch, seq, tm=512, sample=False)
    b_p = _prompt_attention(qt_p, qit_p, wit_p, kib_p, kb_p, vtb_p, batch, seq)
    x2p = _out_ln(a_p, b_p, x1p, w_out_p, ln2_g, ln2_b, tm=512)
    yp = _ffn_ln(x2p, wgu2, wd2, ln3_g, ln3_b, tm=512)

    xs = x_sample.reshape(n_samples, D_MODEL)
    x1s = _ffn_ln(xs, wgu1, wd1, ln1_g, ln1_b, tm=n_samples)
    a_s, kt_s, vt_s, kit_s, q_s, qi_s, kiwi_s, av_s = _mixer_in(
        x1s, w_in_p, *tabs_s, a_ln_g, a_ln_b, ws_sample, bias_sample, 1, n_samples, tm=n_samples, sample=True)
    pt_flat = page_table.reshape(-1)
    kidx_t = jnp.transpose(cache_kidx[l], (0, 2, 1))
    k_t = jnp.transpose(cache_k[l], (0, 2, 3, 1))
    v_t = jnp.transpose(cache_v[l], (0, 2, 3, 1))
    a_idx = jnp.pad(qi_s.reshape(n_samples, N_IDX_HEADS, IDX_DIM), ((0, 0), (0, 16 - N_IDX_HEADS), (0, 0)))
    w_idx = jnp.pad(kiwi_s[:, IDX_DIM:IDX_DIM + N_IDX_HEADS], ((0, 0), (0, 16 - N_IDX_HEADS)))[..., None]
    scores = _sample_scores(pt_flat, a_idx, w_idx, kidx_t, n_samples, n_pages)
    topk = min(TOPK_MAX, (past + 1) // 4)
    bias_s, bias_new = _sample_select(scores.reshape(n_samples, past), qi_s, kiwi_s, topk)
    aq = jnp.pad(q_s.reshape(n_samples, N_KV_HEADS, GROUP_SIZE, HEAD_DIM),
                 ((0, 0), (0, 0), (0, 16 - GROUP_SIZE), (0, 0)))
    knew = jnp.transpose(kt_s[0], (2, 0, 1))[:, :, None, :]
    vnew = jnp.transpose(vt_s[0], (2, 0, 1))[:, :, None, :]
    o_s = _sample_attention(pt_flat, aq, knew, vnew, bias_s.reshape(n_samples, 1, past),
                            bias_new.reshape(n_samples, 1, LANES), k_t, v_t, n_samples, n_pages)
    b_s = o_s[:, :, :GROUP_SIZE, :].reshape(n_samples, N_HEADS * HEAD_DIM).astype(BF16)
    x2s = _out_ln(a_s, b_s, x1s, w_out_p, ln2_g, ln2_b, tm=n_samples)
    ys = _ffn_ln(x2s, wgu2, wd2, ln3_g, ln3_b, tm=n_samples)

    new_k_p = jnp.transpose(kt_p, (0, 3, 1, 2))[None]
    new_v_p = jnp.transpose(vt_p, (0, 3, 1, 2))[None]
    new_ki_p = jnp.transpose(kit_p, (0, 2, 1))[None]
    new_k_s = jnp.transpose(kt_s[0], (2, 0, 1))[None, :, None]
    new_v_s = jnp.transpose(vt_s[0], (2, 0, 1))[None, :, None]
    new_ki_s = jnp.transpose(kit_s[0], (1, 0))[None, :, None]
    return (yp.reshape(batch, seq, D_MODEL), ys.reshape(n_samples, 1, D_MODEL),
            new_k_p, new_v_p, new_ki_p, new_k_s, new_v_s, new_ki_s,
            av_s.reshape(1, n_samples, 1, A_WIDTH))
```

```python
import functools

import jax
import jax.numpy as jnp
from jax import lax
from jax.experimental import pallas as pl
from jax.experimental.pallas import tpu as pltpu

F32 = jnp.float32
BF16 = jnp.bfloat16

D_MODEL = 1024
D_FF = 2816
A_WIDTH = 512
A_GROUPS = 4
CHUNK = 128
HEAD_DIM = 64
N_HEADS = 8
N_KV_HEADS = 2
GROUP_SIZE = N_HEADS // N_KV_HEADS
N_IDX_HEADS = 4
IDX_DIM = 64
IDX_W_SCALE = (N_IDX_HEADS * IDX_DIM) ** -0.5
TOPK_MAX = 256
PAGE_SIZE = 128
ROPE_THETA = 10000.0
LN_EPS = 1e-5
DEPTH = 1
ALPHA = (2.0 * DEPTH) ** 0.25
ATTN_SCALE = HEAD_DIM ** -0.5

IN_SPLITS = (512, 512, 512, 128, 128, 256, 64, 4)
IN_WIDTH = sum(IN_SPLITS)
LANES = 128
IN_PAD = -(-IN_WIDTH // LANES) * LANES
COL_Q = 1024
COL_K = 1536
COL_V = 1664
COL_QI = 1792
COL_KIWI = 2048

FF_CHUNK = 256
N_FF_CHUNKS = D_FF // FF_CHUNK
VMEM_LIMIT = 56 * 1024 * 1024
INT_MIN = -2 ** 31
NEG_INF = float("-inf")


def _layernorm(y, g, b):
    mu = jnp.mean(y, axis=-1, keepdims=True)
    d = y - mu
    var = jnp.mean(d * d, axis=-1, keepdims=True)
    return d * lax.rsqrt(var + LN_EPS) * g + b


def _gelu_tanh(x):
    return 0.5 * x * (1.0 + jnp.tanh(0.7978845608028654 * (x + 0.044715 * (x * x * x))))


def _ffn_ln_kernel(x_ref, wgu_ref, wd_ref, g_ref, b_ref, o_ref):
    x = x_ref[...]
    xb = x.astype(BF16)
    acc = jnp.zeros(x.shape, F32)
    for c in range(N_FF_CHUNKS):
        gu = jnp.dot(xb, wgu_ref[c], preferred_element_type=F32)
        gate = gu[:, :FF_CHUNK]
        up = gu[:, FF_CHUNK:]
        h = (gate * (1.0 / (1.0 + jnp.exp(-gate))) * up).astype(BF16)
        acc = acc + jnp.dot(h, wd_ref[c], preferred_element_type=F32)
    o_ref[...] = _layernorm(ALPHA * x + 0.5 * acc, g_ref[...], b_ref[...])


def _ffn_ln(x, wgu, wd, g, b, tm):
    n = x.shape[0]
    const3 = lambda i: (0, 0, 0)
    const2 = lambda i: (0, 0)
    return pl.pallas_call(
        _ffn_ln_kernel,
        grid=(n // tm,),
        in_specs=[
            pl.BlockSpec((tm, D_MODEL), lambda i: (i, 0)),
            pl.BlockSpec((N_FF_CHUNKS, D_MODEL, 2 * FF_CHUNK), const3),
            pl.BlockSpec((N_FF_CHUNKS, FF_CHUNK, D_MODEL), const3),
            pl.BlockSpec((1, D_MODEL), const2),
            pl.BlockSpec((1, D_MODEL), const2),
        ],
        out_specs=pl.BlockSpec((tm, D_MODEL), lambda i: (i, 0)),
        out_shape=jax.ShapeDtypeStruct((n, D_MODEL), F32),
        compiler_params=pltpu.CompilerParams(
            dimension_semantics=("arbitrary",), vmem_limit_bytes=VMEM_LIMIT),
        name="ffn_ln",
    )(x, wgu, wd, g, b)


def _mixer_in_kernel(x_ref, w_ref, cos_ref, sin_ref, cosk_ref, sink_ref, ag_ref, ab_ref, ws_ref,
                     bias_ref, aout_ref, kt_ref, vt_ref, kit_ref, *mode_refs, sample):
    tm = x_ref.shape[0]
    xb = x_ref[...].astype(BF16)

    def proj(lo, hi):
        return jnp.dot(xb, w_ref[:, lo:hi], preferred_element_type=F32)

    u = _gelu_tanh(proj(0, A_WIDTH))
    vn = _layernorm(_gelu_tanh(proj(A_WIDTH, 2 * A_WIDTH)), ag_ref[...], ab_ref[...])
    vnb = vn.astype(BF16)
    for c in range(tm // CHUNK):
        rows = slice(c * CHUNK, (c + 1) * CHUNK)
        for g in range(A_GROUPS):
            cols = slice(g * LANES, (g + 1) * LANES)
            mixed = jnp.dot(ws_ref[g], vnb[rows, cols], preferred_element_type=F32) + bias_ref[:, cols]
            aout_ref[rows, cols] = (u[rows, cols] * mixed).astype(BF16)

    lane = lax.broadcasted_iota(jnp.int32, (tm, LANES), 1)
    first_half = (lane & (HEAD_DIM - 1)) < (HEAD_DIM // 2)

    def rope(x, cos, sin):
        partner = jnp.where(first_half, pltpu.roll(x, LANES - HEAD_DIM // 2, 1),
                            pltpu.roll(x, HEAD_DIM // 2, 1))
        return x * cos + partner * sin

    cos = cos_ref[...]
    sin = sin_ref[...]
    qall = proj(COL_Q, COL_K)
    rest = proj(COL_K, IN_PAD)
    k = rope(rest[:, 0:128], cos, sin)
    v = rest[:, 128:256]
    kiwi = rope(rest[:, 512:640], cosk_ref[...], sink_ref[...])
    k_t = k.T
    v_t = v.T
    kiwi_t = kiwi.T
    kt_ref[0] = k_t.reshape(N_KV_HEADS, HEAD_DIM, tm)
    vt_ref[0] = v_t.reshape(N_KV_HEADS, HEAD_DIM, tm)
    kit_ref[0] = kiwi_t[0:IDX_DIM]
    if sample:
        q_ref, qi_ref, kiwi_ref, av_ref = mode_refs
        av_ref[...] = vn
        kiwi_ref[...] = kiwi
        for j in range(4):
            cols = slice(j * LANES, (j + 1) * LANES)
            q_ref[:, cols] = (rope(qall[:, cols], cos, sin) * ATTN_SCALE).astype(BF16)
        for j in range(2):
            cols = slice(j * LANES, (j + 1) * LANES)
            qi_ref[:, cols] = rope(rest[:, 256 + j * LANES:256 + (j + 1) * LANES], cos, sin).astype(BF16)
    else:
        q_ref, qit_ref, wit_ref, kib_ref, kdup_ref, vaug_ref = mode_refs
        for j in range(4):
            cols = slice(j * LANES, (j + 1) * LANES)
            q_ref[:, cols] = (rope(qall[:, cols], cos, sin) * ATTN_SCALE).astype(BF16)
        for j in range(2):
            cols = slice(j * LANES, (j + 1) * LANES)
            qit_ref[0, cols, :] = rope(rest[:, 256 + j * LANES:256 + (j + 1) * LANES], cos, sin).T.astype(BF16)
        wit_ref[0] = kiwi_t[IDX_DIM:IDX_DIM + 8]
        kib_ref[...] = kiwi.astype(BF16)
        lo = lane < HEAD_DIM
        k_swapped = pltpu.roll(k, HEAD_DIM, 1)
        kdup_ref[:, 0:LANES] = jnp.where(lo, k, k_swapped).astype(BF16)
        kdup_ref[:, LANES:2 * LANES] = jnp.where(lo, k_swapped, k).astype(BF16)
        vaug_ref[:, 0:LANES] = jnp.where(lo, v, 1.0).astype(BF16)
        vaug_ref[:, LANES:2 * LANES] = jnp.where(lo, pltpu.roll(v, HEAD_DIM, 1), 1.0).astype(BF16)


def _mixer_in(x, w_in, cos, sin, cosk, sink, ag, ab, ws, bias, batch, seq, tm, sample):
    n = batch * seq
    nj = seq // tm
    row = lambda b, j: (b * nj + j, 0)
    tab = lambda b, j: (j, 0)
    const2 = lambda b, j: (0, 0)
    const3 = lambda b, j: (0, 0, 0)
    fmajor = lambda b, j: (b, 0, j)
    n_q = N_HEADS * HEAD_DIM
    n_qi = N_IDX_HEADS * IDX_DIM
    out_shape = [
        jax.ShapeDtypeStruct((n, A_WIDTH), BF16),
        jax.ShapeDtypeStruct((batch, N_KV_HEADS, HEAD_DIM, seq), F32),
        jax.ShapeDtypeStruct((batch, N_KV_HEADS, HEAD_DIM, seq), F32),
        jax.ShapeDtypeStruct((batch, IDX_DIM, seq), F32),
    ]
    out_specs = [
        pl.BlockSpec((tm, A_WIDTH), row),
        pl.BlockSpec((1, N_KV_HEADS, HEAD_DIM, tm), lambda b, j: (b, 0, 0, j)),
        pl.BlockSpec((1, N_KV_HEADS, HEAD_DIM, tm), lambda b, j: (b, 0, 0, j)),
        pl.BlockSpec((1, IDX_DIM, tm), fmajor),
    ]
    if sample:
        out_shape += [
            jax.ShapeDtypeStruct((n, n_q), BF16),
            jax.ShapeDtypeStruct((n, n_qi), BF16),
            jax.ShapeDtypeStruct((n, LANES), F32),
            jax.ShapeDtypeStruct((n, A_WIDTH), F32),
        ]
        out_specs += [
            pl.BlockSpec((tm, n_q), row),
            pl.BlockSpec((tm, n_qi), row),
            pl.BlockSpec((tm, LANES), row),
            pl.BlockSpec((tm, A_WIDTH), row),
        ]
    else:
        out_shape += [
            jax.ShapeDtypeStruct((n, n_q), BF16),
            jax.ShapeDtypeStruct((batch, n_qi, seq), BF16),
            jax.ShapeDtypeStruct((batch, 8, seq), F32),
            jax.ShapeDtypeStruct((n, LANES), BF16),
            jax.ShapeDtypeStruct((n, N_KV_HEADS * LANES), BF16),
            jax.ShapeDtypeStruct((n, N_KV_HEADS * LANES), BF16),
        ]
        out_specs += [
            pl.BlockSpec((tm, n_q), row),
            pl.BlockSpec((1, n_qi, tm), fmajor),
            pl.BlockSpec((1, 8, tm), fmajor),
            pl.BlockSpec((tm, LANES), row),
            pl.BlockSpec((tm, N_KV_HEADS * LANES), row),
            pl.BlockSpec((tm, N_KV_HEADS * LANES), row),
        ]
    return pl.pallas_call(
        functools.partial(_mixer_in_kernel, sample=sample),
        grid=(batch, nj),
        in_specs=[
            pl.BlockSpec((tm, D_MODEL), row),
            pl.BlockSpec((D_MODEL, IN_PAD), const2),
            pl.BlockSpec((tm, LANES), tab),
            pl.BlockSpec((tm, LANES), tab),
            pl.BlockSpec((tm, LANES), tab),
            pl.BlockSpec((tm, LANES), tab),
            pl.BlockSpec((1, A_WIDTH), const2),
            pl.BlockSpec((1, A_WIDTH), const2),
            pl.BlockSpec((A_GROUPS, CHUNK, CHUNK), const3),
            pl.BlockSpec((CHUNK, A_WIDTH), const2),
        ],
        out_specs=out_specs,
        out_shape=out_shape,
        compiler_params=pltpu.CompilerParams(
            dimension_semantics=("arbitrary", "arbitrary"), vmem_limit_bytes=VMEM_LIMIT),
        name="mixer_in",
    )(x, w_in, cos, sin, cosk, sink, ag, ab, ws, bias)


def _sortable_key(s):
    s = jnp.where(s == 0.0, 0.0, s)
    bits = pltpu.bitcast(s, jnp.int32)
    return jnp.where(bits < 0, bits ^ jnp.int32(0x7FFFFFFF), bits)


def _kth_largest_key(count_ge, rows, k):
    kf = jnp.float32(k)
    zero = jnp.zeros((rows, 1), jnp.int32)
    cur = jnp.where(count_ge(zero) >= kf, zero, jnp.int32(INT_MIN))

    def body(it, cur):
        cand = cur | jnp.left_shift(jnp.int32(1), jnp.int32(30) - it)
        return jnp.where(count_ge(cand) >= kf, cand, cur)

    return lax.fori_loop(0, 31, body, cur)


def _upper_tri_ones():
    r = lax.broadcasted_iota(jnp.int32, (LANES, LANES), 0)
    c = lax.broadcasted_iota(jnp.int32, (LANES, LANES), 1)
    return jnp.where(r <= c, 1.0, 0.0).astype(BF16)


def _tie_break_bias(key_ref, bias_ref, tau, need, n_cols):
    tri = _upper_tri_ones()
    rows = tau.shape[0]
    run = jnp.zeros((rows, 1), F32)
    for c in range(n_cols // LANES):
        cols = slice(c * LANES, (c + 1) * LANES)
        kc = key_ref[:, cols]
        eq = jnp.where(kc == tau, 1.0, 0.0)
        rank = run + jnp.dot(eq.astype(BF16), tri, preferred_element_type=F32)
        keep = (kc > tau) | ((kc == tau) & (rank <= need))
        bias_ref[:, cols] = jnp.where(keep, 0.0, NEG_INF)
        run = run + jnp.sum(eq, axis=-1, keepdims=True)
    return run


Q_BLOCK = 256
KEY_CHUNK = 256
INT16_MIN = -2 ** 15
PACK16 = 16
N_COUNT_ACCS = 8


def _count_rows_16(mask_of_group, n_rows, lanes):
    one = jnp.ones((PACK16, lanes), jnp.int16)
    zero = jnp.zeros((PACK16, lanes), jnp.int16)
    accs = [zero] * min(N_COUNT_ACCS, n_rows // PACK16)
    for r in range(n_rows // PACK16):
        accs[r % len(accs)] = accs[r % len(accs)] + jnp.where(mask_of_group(r), one, zero)
    parts = accs
    while len(parts) > 1:
        parts = [parts[j] + parts[j + 1] for j in range(0, len(parts), 2)]
    return jnp.sum(parts[0].astype(jnp.int32), axis=0, keepdims=True)


def _bcast16(x, lanes):
    return jnp.broadcast_to(x, (PACK16, lanes)).astype(jnp.int16)


def _kth_largest_16(ref, n_rows, k):
    lanes = ref.shape[1]

    def body(it, cur):
        cand = cur + jnp.left_shift(jnp.int32(1), jnp.int32(15) - it)
        c16 = _bcast16(cand, lanes)
        cnt = _count_rows_16(lambda r: ref[r * PACK16:(r + 1) * PACK16, :] >= c16, n_rows, lanes)
        return jnp.where(cnt >= k, cand, cur)

    return lax.fori_loop(0, 16, body, jnp.full((1, lanes), INT16_MIN, jnp.int32))


def _kth_largest_key_rows(hi_ref, lo_ref, n_rows, k):
    lanes = hi_ref.shape[1]
    tau_hi = _kth_largest_16(hi_ref, n_rows, jnp.int32(k))
    t16 = _bcast16(tau_hi, lanes)
    n_above = _count_rows_16(lambda r: hi_ref[r * PACK16:(r + 1) * PACK16, :] > t16, n_rows, lanes)
    min16 = jnp.full((PACK16, lanes), INT16_MIN, jnp.int16)
    for r in range(n_rows // PACK16):
        rows = slice(r * PACK16, (r + 1) * PACK16)
        lo_ref[rows, :] = jnp.where(hi_ref[rows, :] == t16, lo_ref[rows, :], min16)
    tau_lo = _kth_largest_16(lo_ref, n_rows, jnp.int32(k) - n_above)
    return jnp.left_shift(tau_hi, 16) | (tau_lo - INT16_MIN)


def _lower_tri_ones():
    r = lax.broadcasted_iota(jnp.int32, (LANES, LANES), 0)
    c = lax.broadcasted_iota(jnp.int32, (LANES, LANES), 1)
    return jnp.where(c <= r, 1.0, 0.0).astype(BF16)


def _chunk_rows(c):
    return pl.ds(pl.multiple_of(c * KEY_CHUNK, KEY_CHUNK), KEY_CHUNK)


def _select_topk_bias(i, n_chunks, qit_ref, wit_ref, kib_ref, key_ref, hi_ref, lo_ref, tau_ref, biasq_ref):
    tq = Q_BLOCK
    zeros64 = jnp.zeros((HEAD_DIM, tq), BF16)
    qi_rhs = jnp.concatenate(
        [jnp.concatenate([qit_ref[0, h * IDX_DIM:(h + 1) * IDX_DIM, :], zeros64], axis=0)
         for h in range(N_IDX_HEADS)], axis=1)
    wi = wit_ref[0]
    q_pos = i * tq + lax.broadcasted_iota(jnp.int32, (KEY_CHUNK, tq), 1)
    row_iota = lax.broadcasted_iota(jnp.int32, (KEY_CHUNK, tq), 0)

    def keys_of(c, carry):
        dots = jnp.dot(kib_ref[_chunk_rows(c), :], qi_rhs, preferred_element_type=F32)
        s = jnp.zeros((KEY_CHUNK, tq), F32)
        for h in range(N_IDX_HEADS):
            s = s + wi[h:h + 1, :] * jnp.maximum(dots[:, h * tq:(h + 1) * tq], 0.0)
        visible = (c * KEY_CHUNK + row_iota) <= q_pos
        key = _sortable_key(jnp.where(visible, s, NEG_INF))
        rows = _chunk_rows(c)
        key_ref[rows, :] = key
        hi_ref[rows, :] = (key >> 16).astype(jnp.int16)
        lo_ref[rows, :] = ((key & 0xFFFF) + INT16_MIN).astype(jnp.int16)
        return carry

    lax.fori_loop(0, n_chunks, keys_of, 0)

    for c in range(1, key_ref.shape[0] // KEY_CHUNK):
        @pl.when(i == c)
        def _(c=c):
            tau_c = _kth_largest_key_rows(hi_ref, lo_ref, (c + 1) * KEY_CHUNK, TOPK_MAX)
            tau_ref[...] = jnp.broadcast_to(tau_c, tau_ref.shape)

    tau = tau_ref[0:1, :]

    def bias_of(c, counts):
        n_ge, n_gt = counts
        key = key_ref[_chunk_rows(c), :]
        ge = key >= tau
        biasq_ref[c] = jnp.where(ge, 0.0, NEG_INF).T
        n_ge = n_ge + jnp.sum(jnp.where(ge, 1.0, 0.0), axis=0, keepdims=True)
        n_gt = n_gt + jnp.sum(jnp.where(key > tau, 1.0, 0.0), axis=0, keepdims=True)
        return n_ge, n_gt

    zero = jnp.zeros((1, tq), F32)
    n_ge, n_gt = lax.fori_loop(0, n_chunks, bias_of, (zero, zero))

    @pl.when(jnp.max(n_ge) > jnp.float32(TOPK_MAX))
    def _():
        need = jnp.float32(TOPK_MAX) - n_gt
        tri = _lower_tri_ones()

        def tie_of(c, run):
            halves = []
            for half in range(KEY_CHUNK // LANES):
                rows = pl.ds(pl.multiple_of(c * KEY_CHUNK + half * LANES, LANES), LANES)
                kc = key_ref[rows, :]
                eq = jnp.where(kc == tau, 1.0, 0.0)
                rank = run + jnp.dot(tri, eq.astype(BF16), preferred_element_type=F32)
                keep = (kc > tau) | ((kc == tau) & (rank <= need))
                halves.append(jnp.where(keep, 0.0, NEG_INF))
                run = run + jnp.sum(eq, axis=0, keepdims=True)
            biasq_ref[c] = jnp.concatenate(halves, axis=0).T
            return run

        lax.fori_loop(0, n_chunks, tie_of, zero)


def _prompt_attn_kernel(q_ref, qit_ref, wit_ref, kib_ref, kdup_ref, vaug_ref, o_ref,
                        key_ref, hi_ref, lo_ref, tau_ref, biasq_ref, lg_ref, m_ref, acc_ref):
    i = pl.program_id(1)
    n_chunks = i + 1
    tq = Q_BLOCK

    @pl.when(i == 0)
    def _():
        k_pos = lax.broadcasted_iota(jnp.int32, (tq, KEY_CHUNK), 1)
        q_pos = lax.broadcasted_iota(jnp.int32, (tq, KEY_CHUNK), 0)
        biasq_ref[0] = jnp.where(k_pos <= q_pos, 0.0, NEG_INF)

    @pl.when(i > 0)
    def _():
        _select_topk_bias(i, n_chunks, qit_ref, wit_ref, kib_ref, key_ref, hi_ref, lo_ref, tau_ref, biasq_ref)

    lane = lax.broadcasted_iota(jnp.int32, (tq, LANES), 1)
    lo = lane < HEAD_DIM
    q4 = []
    for g in range(N_KV_HEADS):
        parts = []
        for j in range(GROUP_SIZE // 2):
            blk = q_ref[:, g * 2 * LANES + j * LANES:g * 2 * LANES + (j + 1) * LANES].astype(F32)
            parts.append(jnp.where(lo, blk, 0.0).astype(BF16))
            parts.append(jnp.where(lo, 0.0, blk).astype(BF16))
        q4.append(jnp.concatenate(parts, axis=0))
    m_ref[...] = jnp.full(m_ref.shape, NEG_INF, F32)

    def mask_and_max(c, carry):
        bias = biasq_ref[c]
        for g in range(N_KV_HEADS):
            raw = lax.dot_general(q4[g], kdup_ref[_chunk_rows(c), g * LANES:(g + 1) * LANES],
                                  (((1,), (1,)), ((), ())), preferred_element_type=F32)
            for r in range(GROUP_SIZE):
                rows = slice(r * tq, (r + 1) * tq)
                lg = raw[rows] + bias
                lg_ref[g, c, rows, :] = lg
                m_ref[g, rows, :] = jnp.maximum(m_ref[g, rows, :], jnp.maximum(lg[:, :LANES], lg[:, LANES:]))
        return carry

    lax.fori_loop(0, n_chunks, mask_and_max, 0)
    m_ref[...] = jnp.broadcast_to(jnp.max(m_ref[...], axis=-1, keepdims=True), m_ref.shape)
    acc_ref[...] = jnp.zeros(acc_ref.shape, F32)

    def accumulate(c, carry):
        for g in range(N_KV_HEADS):
            m = m_ref[g]
            lg = lg_ref[g, c]
            p = jnp.concatenate([jnp.exp(lg[:, :LANES] - m), jnp.exp(lg[:, LANES:] - m)], axis=1).astype(BF16)
            acc_ref[g] += jnp.dot(p, vaug_ref[_chunk_rows(c), g * LANES:(g + 1) * LANES],
                                  preferred_element_type=F32)
        return carry

    lax.fori_loop(0, n_chunks, accumulate, 0)
    for g in range(N_KV_HEADS):
        acc = acc_ref[g]
        out = acc * (1.0 / pltpu.roll(acc, HEAD_DIM, 1))
        for j in range(GROUP_SIZE // 2):
            even = out[(2 * j) * tq:(2 * j + 1) * tq]
            odd = pltpu.roll(out[(2 * j + 1) * tq:(2 * j + 2) * tq], HEAD_DIM, 1)
            o_ref[:, g * 2 * LANES + j * LANES:g * 2 * LANES + (j + 1) * LANES] = jnp.where(
                lo, even, odd).astype(BF16)


def _prompt_attention(q, qit, wit, kib, kdup, vaug, batch, seq):
    nb = seq // Q_BLOCK
    n_q = N_HEADS * HEAD_DIM
    n_qi = N_IDX_HEADS * IDX_DIM
    n_kc = seq // KEY_CHUNK
    qblk = lambda b, i: (b, 0, i)
    perb = lambda b, i: (b, 0)
    return pl.pallas_call(
        _prompt_attn_kernel,
        grid=(batch, nb),
        in_specs=[
            pl.BlockSpec((Q_BLOCK, n_q), lambda b, i: (b * nb + i, 0)),
            pl.BlockSpec((1, n_qi, Q_BLOCK), qblk),
            pl.BlockSpec((1, 8, Q_BLOCK), qblk),
            pl.BlockSpec((seq, LANES), perb),
            pl.BlockSpec((seq, N_KV_HEADS * LANES), perb),
            pl.BlockSpec((seq, N_KV_HEADS * LANES), perb),
        ],
        out_specs=pl.BlockSpec((Q_BLOCK, n_q), lambda b, i: (b * nb + i, 0)),
        out_shape=jax.ShapeDtypeStruct((batch * seq, n_q), BF16),
        scratch_shapes=[
            pltpu.VMEM((seq, Q_BLOCK), jnp.int32),
            pltpu.VMEM((seq, Q_BLOCK), jnp.int16),
            pltpu.VMEM((seq, Q_BLOCK), jnp.int16),
            pltpu.VMEM((8, Q_BLOCK), jnp.int32),
            pltpu.VMEM((n_kc, Q_BLOCK, KEY_CHUNK), F32),
            pltpu.VMEM((N_KV_HEADS, n_kc, GROUP_SIZE * Q_BLOCK, KEY_CHUNK), F32),
            pltpu.VMEM((N_KV_HEADS, GROUP_SIZE * Q_BLOCK, LANES), F32),
            pltpu.VMEM((N_KV_HEADS, GROUP_SIZE * Q_BLOCK, LANES), F32),
        ],
        compiler_params=pltpu.CompilerParams(
            dimension_semantics=("arbitrary", "arbitrary"), vmem_limit_bytes=VMEM_LIMIT),
        name="prompt_attention",
    )(q, qit, wit, kib, kdup, vaug)


def _page_copies(pt_ref, b, n_pages, src_hbm, dst_for_page, sem):
    return [pltpu.make_async_copy(src_hbm.at[pt_ref[b * n_pages + p]], dst_for_page(p), sem)
            for p in range(n_pages)]


def _sample_scores_kernel(pt_ref, a_ref, w_ref, kidx_hbm, o_ref, buf, sem, *, n_pages):
    b = pl.program_id(0)
    nb = pl.num_programs(0)
    slot = b % 2

    def copies(bb, sl):
        return _page_copies(pt_ref, bb, n_pages, kidx_hbm,
                            lambda p: buf.at[sl, :, pl.ds(p * PAGE_SIZE, PAGE_SIZE)], sem.at[sl])

    @pl.when(b == 0)
    def _():
        for cp in copies(b, slot):
            cp.start()

    @pl.when(b + 1 < nb)
    def _():
        for cp in copies(b + 1, 1 - slot):
            cp.start()

    for cp in copies(b, slot):
        cp.wait()
    dots = jnp.dot(a_ref[0], buf[slot].astype(BF16), preferred_element_type=F32)
    o_ref[0] = jnp.sum(jnp.maximum(dots, 0.0) * w_ref[0], axis=0, keepdims=True)


def _sample_scores(pt_flat, a, w, kidx_t, n_samples, n_pages):
    past = n_pages * PAGE_SIZE
    return pl.pallas_call(
        functools.partial(_sample_scores_kernel, n_pages=n_pages),
        grid_spec=pltpu.PrefetchScalarGridSpec(
            num_scalar_prefetch=1,
            grid=(n_samples,),
            in_specs=[
                pl.BlockSpec((1, 16, IDX_DIM), lambda b, pt: (b, 0, 0)),
                pl.BlockSpec((1, 16, 1), lambda b, pt: (b, 0, 0)),
                pl.BlockSpec(memory_space=pl.ANY),
            ],
            out_specs=pl.BlockSpec((1, 1, past), lambda b, pt: (b, 0, 0)),
            scratch_shapes=[pltpu.VMEM((2, IDX_DIM, past), F32), pltpu.SemaphoreType.DMA((2,))],
        ),
        out_shape=jax.ShapeDtypeStruct((n_samples, 1, past), F32),
        compiler_params=pltpu.CompilerParams(
            dimension_semantics=("arbitrary",), vmem_limit_bytes=VMEM_LIMIT),
        name="sample_scores",
    )(pt_flat, a, w, kidx_t)


def _sample_select_kernel(s_ref, qi_ref, kiwi_ref, bias_ref, bias_new_ref, key_ref, *, topk):
    rows, past = s_ref.shape
    lane = lax.broadcasted_iota(jnp.int32, (rows, LANES), 1)
    lo = lane < IDX_DIM
    kiwi = kiwi_ref[...]
    kib = kiwi.astype(BF16).astype(F32)
    kidup = jnp.where(lo, kib, pltpu.roll(kib, IDX_DIM, 1))
    s_new = jnp.zeros((rows, 1), F32)
    for j in range(2):
        t = qi_ref[:, j * LANES:(j + 1) * LANES].astype(F32) * kidup
        for half in range(2):
            d = jnp.sum(jnp.where(lo if half == 0 else jnp.logical_not(lo), t, 0.0), axis=-1, keepdims=True)
            h = 2 * j + half
            s_new = s_new + kiwi[:, IDX_DIM + h:IDX_DIM + h + 1] * jnp.maximum(d, 0.0)
    key_new = _sortable_key(s_new)
    key_ref[...] = _sortable_key(s_ref[...])

    def count_ge(c):
        return (jnp.sum(jnp.where(key_ref[...] >= c, 1.0, 0.0), axis=-1, keepdims=True)
                + jnp.where(key_new >= c, 1.0, 0.0))

    tau = _kth_largest_key(count_ge, rows, topk)
    key = key_ref[...]
    bias_ref[...] = jnp.where(key >= tau, 0.0, NEG_INF)
    n_gt = (jnp.sum(jnp.where(key > tau, 1.0, 0.0), axis=-1, keepdims=True)
            + jnp.where(key_new > tau, 1.0, 0.0))
    need = jnp.float32(topk) - n_gt
    bias_new_ref[...] = jnp.broadcast_to(jnp.where(key_new >= tau, 0.0, NEG_INF), (rows, LANES))

    @pl.when(jnp.max(count_ge(tau)) > jnp.float32(topk))
    def _():
        n_eq_past = _tie_break_bias(key_ref, bias_ref, tau, need, past)
        keep_new = (key_new > tau) | ((key_new == tau) & (n_eq_past + 1.0 <= need))
        bias_new_ref[...] = jnp.broadcast_to(jnp.where(keep_new, 0.0, NEG_INF), (rows, LANES))


def _sample_select(scores, qi, kiwi, topk):
    rows, past = scores.shape
    return pl.pallas_call(
        functools.partial(_sample_select_kernel, topk=topk),
        out_shape=[jax.ShapeDtypeStruct((rows, past), F32), jax.ShapeDtypeStruct((rows, LANES), F32)],
        scratch_shapes=[pltpu.VMEM((rows, past), jnp.int32)],
        compiler_params=pltpu.CompilerParams(vmem_limit_bytes=VMEM_LIMIT),
        name="sample_select",
    )(scores, qi, kiwi)


def _sample_attn_kernel(pt_ref, aq_ref, knew_ref, vnew_ref, bias_ref, bias_new_ref, k_hbm, v_hbm, o_ref,
                        kbuf, vbuf, sem, *, n_pages):
    b = pl.program_id(0)
    nb = pl.num_programs(0)
    slot = b % 2

    def copies(bb, sl):
        dst_k = lambda p: kbuf.at[sl, :, :, pl.ds(p * PAGE_SIZE, PAGE_SIZE)]
        dst_v = lambda p: vbuf.at[sl, :, :, pl.ds(p * PAGE_SIZE, PAGE_SIZE)]
        return (_page_copies(pt_ref, bb, n_pages, k_hbm, dst_k, sem.at[0, sl])
                + _page_copies(pt_ref, bb, n_pages, v_hbm, dst_v, sem.at[1, sl]))

    @pl.when(b == 0)
    def _():
        for cp in copies(b, slot):
            cp.start()

    @pl.when(b + 1 < nb)
    def _():
        for cp in copies(b + 1, 1 - slot):
            cp.start()

    for cp in copies(b, slot):
        cp.wait()
    bias = bias_ref[0]
    bias_new = bias_new_ref[0][:, 0:1]
    for g in range(N_KV_HEADS):
        aq = aq_ref[0, g]
        logits = jnp.dot(aq, kbuf[slot, g].astype(BF16), preferred_element_type=F32) + bias
        knew = knew_ref[0, g].astype(BF16).astype(F32)
        l_new = jnp.sum(aq.astype(F32) * knew, axis=-1, keepdims=True) + bias_new
        m = jnp.maximum(jnp.max(logits, axis=-1, keepdims=True), l_new)
        p = jnp.exp(logits - m)
        p_new = jnp.exp(l_new - m)
        l = jnp.sum(p, axis=-1, keepdims=True) + p_new
        pv = lax.dot_general(p.astype(BF16), vbuf[slot, g].astype(BF16), (((1,), (1,)), ((), ())),
                             preferred_element_type=F32)
        pv = pv + p_new.astype(BF16).astype(F32) * vnew_ref[0, g].astype(BF16).astype(F32)
        o_ref[0, g] = pv * (1.0 / l)


def _sample_attention(pt_flat, aq, knew, vnew, bias, bias_new, k_t, v_t, n_samples, n_pages):
    past = n_pages * PAGE_SIZE
    blk4 = lambda b, pt: (b, 0, 0, 0)
    blk3 = lambda b, pt: (b, 0, 0)
    return pl.pallas_call(
        functools.partial(_sample_attn_kernel, n_pages=n_pages),
        grid_spec=pltpu.PrefetchScalarGridSpec(
            num_scalar_prefetch=1,
            grid=(n_samples,),
            in_specs=[
                pl.BlockSpec((1, N_KV_HEADS, 16, HEAD_DIM), blk4),
                pl.BlockSpec((1, N_KV_HEADS, 1, HEAD_DIM), blk4),
                pl.BlockSpec((1, N_KV_HEADS, 1, HEAD_DIM), blk4),
                pl.BlockSpec((1, 1, past), blk3),
                pl.BlockSpec((1, 1, LANES), blk3),
                pl.BlockSpec(memory_space=pl.ANY),
                pl.BlockSpec(memory_space=pl.ANY),
            ],
            out_specs=pl.BlockSpec((1, N_KV_HEADS, 16, HEAD_DIM), blk4),
            scratch_shapes=[
                pltpu.VMEM((2, N_KV_HEADS, HEAD_DIM, past), F32),
                pltpu.VMEM((2, N_KV_HEADS, HEAD_DIM, past), F32),
                pltpu.SemaphoreType.DMA((2, 2)),
            ],
        ),
        out_shape=jax.ShapeDtypeStruct((n_samples, N_KV_HEADS, 16, HEAD_DIM), F32),
        compiler_params=pltpu.CompilerParams(
            dimension_semantics=("arbitrary",), vmem_limit_bytes=VMEM_LIMIT),
        name="sample_attention",
    )(pt_flat, aq, knew, vnew, bias, bias_new, k_t, v_t)


def _out_ln_kernel(a_ref, b_ref, x_ref, w_ref, g_ref, beta_ref, o_ref):
    mix = (jnp.dot(a_ref[...], w_ref[0], preferred_element_type=F32)
           + jnp.dot(b_ref[...], w_ref[1], preferred_element_type=F32))
    o_ref[...] = _layernorm(ALPHA * x_ref[...] + mix, g_ref[...], beta_ref[...])


def _out_ln(a, b, x, w, g, beta, tm):
    n = x.shape[0]
    row = lambda i: (i, 0)
    const2 = lambda i: (0, 0)
    return pl.pallas_call(
        _out_ln_kernel,
        grid=(n // tm,),
        in_specs=[
            pl.BlockSpec((tm, A_WIDTH), row),
            pl.BlockSpec((tm, N_HEADS * HEAD_DIM), row),
            pl.BlockSpec((tm, D_MODEL), row),
            pl.BlockSpec((2, A_WIDTH, D_MODEL), lambda i: (0, 0, 0)),
            pl.BlockSpec((1, D_MODEL), const2),
            pl.BlockSpec((1, D_MODEL), const2),
        ],
        out_specs=pl.BlockSpec((tm, D_MODEL), row),
        out_shape=jax.ShapeDtypeStruct((n, D_MODEL), F32),
        compiler_params=pltpu.CompilerParams(
            dimension_semantics=("arbitrary",), vmem_limit_bytes=VMEM_LIMIT),
        name="out_ln",
    )(a, b, x, w, g, beta)


def _rope_tables(pos):
    half = HEAD_DIM // 2
    inv = ROPE_THETA ** (-jnp.arange(half, dtype=F32) / half)
    ang = pos.astype(F32)[:, None] * inv[None, :]
    c = jnp.cos(ang)
    s = jnp.sin(ang)
    cos64 = jnp.concatenate([c, c], axis=1)
    sin64 = jnp.concatenate([-s, s], axis=1)
    n = pos.shape[0]
    cos = jnp.concatenate([cos64, cos64], axis=1)
    sin = jnp.concatenate([sin64, sin64], axis=1)
    cosk = jnp.concatenate([cos64, jnp.full((n, LANES - IDX_DIM), IDX_W_SCALE, F32)], axis=1)
    sink = jnp.concatenate([sin64, jnp.zeros((n, LANES - IDX_DIM), F32)], axis=1)
    return cos, sin, cosk, sink


def _prep_ffn_weights(w_up, w_down):
    gate = w_up[:, :D_FF].reshape(D_MODEL, N_FF_CHUNKS, FF_CHUNK)
    up = w_up[:, D_FF:].reshape(D_MODEL, N_FF_CHUNKS, FF_CHUNK)
    wgu = jnp.concatenate([gate, up], axis=2).transpose(1, 0, 2).astype(BF16)
    wd = w_down.reshape(N_FF_CHUNKS, FF_CHUNK, D_MODEL).astype(BF16)
    return wgu, wd


def kernel(x_prompt, x_sample, cache_k, cache_v, cache_kidx, page_table, ln1_g, ln1_b, ffn1_w_up, ffn1_w_down, ln2_g, ln2_b, w_in, a_ln_g, a_ln_b, a_ws, a_bs, w_out, ln3_g, ln3_b, ffn2_w_up, ffn2_w_down):
    batch, seq, _ = x_prompt.shape
    n_samples = x_sample.shape[0]
    n_pages = page_table.shape[1]
    past = n_pages * PAGE_SIZE
    l = 0

    wgu1, wd1 = _prep_ffn_weights(ffn1_w_up[l], ffn1_w_down[l])
    wgu2, wd2 = _prep_ffn_weights(ffn2_w_up[l], ffn2_w_down[l])
    w_in_p = jnp.pad(w_in[l], ((0, 0), (0, IN_PAD - IN_WIDTH))).astype(BF16)
    w_out_p = w_out[l].reshape(2, A_WIDTH, D_MODEL).astype(BF16)
    causal = jnp.tril(jnp.ones((CHUNK, CHUNK), dtype=bool))
    ws_prompt = jnp.where(causal[None], a_ws[l], 0).astype(BF16)
    bias_prompt = jnp.repeat(a_bs[l].T, A_WIDTH // A_GROUPS, axis=1)
    eye = jnp.eye(CHUNK, dtype=F32)
    ws_sample = (a_ws[l][:, 0, 0][:, None, None] * eye[None]).astype(BF16)
    bias_sample = jnp.broadcast_to(bias_prompt[0:1], (CHUNK, A_WIDTH))

    pos_p = jnp.arange(seq, dtype=jnp.int32)
    pos_s = jnp.full((n_samples,), past, dtype=jnp.int32)
    tabs_p = _rope_tables(pos_p)
    tabs_s = _rope_tables(pos_s)

    xp = x_prompt.reshape(batch * seq, D_MODEL)
    x1p = _ffn_ln(xp, wgu1, wd1, ln1_g, ln1_b, tm=512)
    a_p, kt_p, vt_p, kit_p, q_p, qit_p, wit_p, kib_p, kdup_p, vaug_p = _mixer_in(
        x1p, w_in_p, *tabs_p, a_ln_g, a_ln_b, ws_prompt, bias_prompt, batch, seq, tm=512, sample=False)
    b_p = _prompt_attention(q_p, qit_p, wit_p, kib_p, kdup_p, vaug_p, batch, seq)
    x2p = _out_ln(a_p, b_p, x1p, w_out_p, ln2_g, ln2_b, tm=512)
    yp = _ffn_ln(x2p, wgu2, wd2, ln3_g, ln3_b, tm=512)

    xs = x_sample.reshape(n_samples, D_MODEL)
    x1s = _ffn_ln(xs, wgu1, wd1, ln1_g, ln1_b, tm=n_samples)
    a_s, kt_s, vt_s, kit_s, q_s, qi_s, kiwi_s, av_s = _mixer_in(
        x1s, w_in_p, *tabs_s, a_ln_g, a_ln_b, ws_sample, bias_sample, 1, n_samples, tm=n_samples, sample=True)
    pt_flat = page_table.reshape(-1)
    kidx_t = jnp.transpose(cache_kidx[l], (0, 2, 1))
    k_t = jnp.transpose(cache_k[l], (0, 2, 3, 1))
    v_t = jnp.transpose(cache_v[l], (0, 2, 3, 1))
    a_idx = jnp.pad(qi_s.reshape(n_samples, N_IDX_HEADS, IDX_DIM), ((0, 0), (0, 16 - N_IDX_HEADS), (0, 0)))
    w_idx = jnp.pad(kiwi_s[:, IDX_DIM:IDX_DIM + N_IDX_HEADS], ((0, 0), (0, 16 - N_IDX_HEADS)))[..., None]
    scores = _sample_scores(pt_flat, a_idx, w_idx, kidx_t, n_samples, n_pages)
    topk = min(TOPK_MAX, (past + 1) // 4)
    bias_s, bias_new = _sample_select(scores.reshape(n_samples, past), qi_s, kiwi_s, topk)
    aq = jnp.pad(q_s.reshape(n_samples, N_KV_HEADS, GROUP_SIZE, HEAD_DIM),
                 ((0, 0), (0, 0), (0, 16 - GROUP_SIZE), (0, 0)))
    knew = jnp.transpose(kt_s[0], (2, 0, 1))[:, :, None, :]
    vnew = jnp.transpose(vt_s[0], (2, 0, 1))[:, :, None, :]
    o_s = _sample_attention(pt_flat, aq, knew, vnew, bias_s.reshape(n_samples, 1, past),
                            bias_new.reshape(n_samples, 1, LANES), k_t, v_t, n_samples, n_pages)
    b_s = o_s[:, :, :GROUP_SIZE, :].reshape(n_samples, N_HEADS * HEAD_DIM).astype(BF16)
    x2s = _out_ln(a_s, b_s, x1s, w_out_p, ln2_g, ln2_b, tm=n_samples)
    ys = _ffn_ln(x2s, wgu2, wd2, ln3_g, ln3_b, tm=n_samples)

    new_k_p = jnp.transpose(kt_p, (0, 3, 1, 2))[None]
    new_v_p = jnp.transpose(vt_p, (0, 3, 1, 2))[None]
    new_ki_p = jnp.transpose(kit_p, (0, 2, 1))[None]
    new_k_s = jnp.transpose(kt_s[0], (2, 0, 1))[None, :, None]
    new_v_s = jnp.transpose(vt_s[0], (2, 0, 1))[None, :, None]
    new_ki_s = jnp.transpose(kit_s[0], (1, 0))[None, :, None]
    return (yp.reshape(batch, seq, D_MODEL), ys.reshape(n_samples, 1, D_MODEL),
            new_k_p, new_v_p, new_ki_p, new_k_s, new_v_s, new_ki_s,
            av_s.reshape(1, n_samples, 1, A_WIDTH))
```

```python
import functools

import jax
import jax.numpy as jnp
from jax import lax
from jax.experimental import pallas as pl
from jax.experimental.pallas import tpu as pltpu

F32 = jnp.float32
BF16 = jnp.bfloat16

D_MODEL = 1024
D_FF = 2816
A_WIDTH = 512
A_GROUPS = 4
CHUNK = 128
HEAD_DIM = 64
N_HEADS = 8
N_KV_HEADS = 2
GROUP_SIZE = N_HEADS // N_KV_HEADS
N_IDX_HEADS = 4
IDX_DIM = 64
IDX_W_SCALE = (N_IDX_HEADS * IDX_DIM) ** -0.5
TOPK_MAX = 256
PAGE_SIZE = 128
ROPE_THETA = 10000.0
LN_EPS = 1e-5
DEPTH = 1
ALPHA = (2.0 * DEPTH) ** 0.25
ATTN_SCALE = HEAD_DIM ** -0.5

IN_SPLITS = (512, 512, 512, 128, 128, 256, 64, 4)
IN_WIDTH = sum(IN_SPLITS)
LANES = 128
IN_PAD = -(-IN_WIDTH // LANES) * LANES
COL_Q = 1024
COL_K = 1536
COL_V = 1664
COL_QI = 1792
COL_KIWI = 2048

Q_BLOCK = 256
KEY_CHUNK = 256
FF_CHUNK = 256
N_FF_CHUNKS = D_FF // FF_CHUNK
VMEM_LIMIT = 56 * 1024 * 1024
INT_MIN = -2 ** 31
NEG_INF = float("-inf")


def _layernorm(y, g, b):
    mu = jnp.mean(y, axis=-1, keepdims=True)
    d = y - mu
    var = jnp.mean(d * d, axis=-1, keepdims=True)
    return d * lax.rsqrt(var + LN_EPS) * g + b


def _gelu_tanh(x):
    return 0.5 * x * (1.0 + jnp.tanh(0.7978845608028654 * (x + 0.044715 * (x * x * x))))


def _swiglu_ln(x, wup_ref, wd_ref, g_ref, b_ref):
    xb = x.astype(BF16)
    acc = jnp.zeros(x.shape, F32)
    for c in range(N_FF_CHUNKS):
        cols = slice(c * FF_CHUNK, (c + 1) * FF_CHUNK)
        gate = jnp.dot(xb, wup_ref[:, cols], preferred_element_type=F32)
        up = jnp.dot(xb, wup_ref[:, D_FF + c * FF_CHUNK:D_FF + (c + 1) * FF_CHUNK], preferred_element_type=F32)
        h = (gate * (1.0 / (1.0 + jnp.exp(-gate))) * up).astype(BF16)
        acc = acc + jnp.dot(h, wd_ref[cols, :], preferred_element_type=F32)
    return _layernorm(ALPHA * x + 0.5 * acc, g_ref[...], b_ref[...])


def _ffn_ln_kernel(x_ref, wgu_ref, wd_ref, g_ref, b_ref, o_ref):
    o_ref[...] = _swiglu_ln(x_ref[...], wgu_ref, wd_ref, g_ref, b_ref)


def _mix_ffn_ln_kernel(a_ref, b_ref, x_ref, wo_ref, g2_ref, b2_ref, wgu_ref, wd_ref, g3_ref, b3_ref, o_ref):
    mix = (jnp.dot(a_ref[...], wo_ref[0], preferred_element_type=F32)
           + jnp.dot(b_ref[...], wo_ref[1], preferred_element_type=F32))
    x2 = _layernorm(ALPHA * x_ref[...] + mix, g2_ref[...], b2_ref[...])
    o_ref[...] = _swiglu_ln(x2, wgu_ref, wd_ref, g3_ref, b3_ref)


def _resident(shape):
    return pl.BlockSpec(shape, lambda i: (0,) * len(shape), pipeline_mode=pl.Buffered(1))


def _ffn_ln(x, wgu, wd, g, b, tm):
    n = x.shape[0]
    row = lambda i: (i, 0)
    return pl.pallas_call(
        _ffn_ln_kernel,
        grid=(n // tm,),
        in_specs=[
            pl.BlockSpec((tm, D_MODEL), row),
            _resident((D_MODEL, 2 * D_FF)),
            _resident((D_FF, D_MODEL)),
            _resident((1, D_MODEL)),
            _resident((1, D_MODEL)),
        ],
        out_specs=pl.BlockSpec((tm, D_MODEL), row),
        out_shape=jax.ShapeDtypeStruct((n, D_MODEL), F32),
        compiler_params=pltpu.CompilerParams(
            dimension_semantics=("arbitrary",), vmem_limit_bytes=VMEM_LIMIT),
        name="ffn_ln",
    )(x, wgu, wd, g, b)


def _mix_ffn_ln(a, b, x, wo, g2, b2, wgu, wd, g3, b3, tm):
    n = x.shape[0]
    row = lambda i: (i, 0)
    return pl.pallas_call(
        _mix_ffn_ln_kernel,
        grid=(n // tm,),
        in_specs=[
            pl.BlockSpec((tm, A_WIDTH), row),
            pl.BlockSpec((tm, N_HEADS * HEAD_DIM), row),
            pl.BlockSpec((tm, D_MODEL), row),
            _resident((2, A_WIDTH, D_MODEL)),
            _resident((1, D_MODEL)),
            _resident((1, D_MODEL)),
            _resident((D_MODEL, 2 * D_FF)),
            _resident((D_FF, D_MODEL)),
            _resident((1, D_MODEL)),
            _resident((1, D_MODEL)),
        ],
        out_specs=pl.BlockSpec((tm, D_MODEL), row),
        out_shape=jax.ShapeDtypeStruct((n, D_MODEL), F32),
        compiler_params=pltpu.CompilerParams(
            dimension_semantics=("arbitrary",), vmem_limit_bytes=VMEM_LIMIT),
        name="mix_ffn_ln",
    )(a, b, x, wo, g2, b2, wgu, wd, g3, b3)


def _mixer_in_kernel(x_ref, w_ref, cos_ref, sin_ref, cosk_ref, sink_ref, ag_ref, ab_ref, ws_ref,
                     bias_ref, aout_ref, kt_ref, vt_ref, kit_ref, *mode_refs, sample):
    tm = x_ref.shape[0]
    xb = x_ref[...].astype(BF16)

    def proj(lo, hi):
        return jnp.dot(xb, w_ref[:, lo:hi], preferred_element_type=F32)

    u = _gelu_tanh(proj(0, A_WIDTH))
    vn = _layernorm(_gelu_tanh(proj(A_WIDTH, 2 * A_WIDTH)), ag_ref[...], ab_ref[...])
    vnb = vn.astype(BF16)
    for c in range(tm // CHUNK):
        rows = slice(c * CHUNK, (c + 1) * CHUNK)
        for g in range(A_GROUPS):
            cols = slice(g * LANES, (g + 1) * LANES)
            mixed = jnp.dot(ws_ref[g], vnb[rows, cols], preferred_element_type=F32) + bias_ref[:, cols]
            aout_ref[rows, cols] = (u[rows, cols] * mixed).astype(BF16)

    lane = lax.broadcasted_iota(jnp.int32, (tm, LANES), 1)
    first_half = (lane & (HEAD_DIM - 1)) < (HEAD_DIM // 2)

    def rope(x, cos, sin):
        partner = jnp.where(first_half, pltpu.roll(x, LANES - HEAD_DIM // 2, 1),
                            pltpu.roll(x, HEAD_DIM // 2, 1))
        return x * cos + partner * sin

    cos = cos_ref[...]
    sin = sin_ref[...]
    qall = proj(COL_Q, COL_K)
    rest = proj(COL_K, IN_PAD)
    k = rope(rest[:, 0:128], cos, sin)
    v = rest[:, 128:256]
    kiwi = rope(rest[:, 512:640], cosk_ref[...], sink_ref[...])
    k_t = k.T
    v_t = v.T
    kiwi_t = kiwi.T
    kt_ref[0] = k_t.reshape(N_KV_HEADS, HEAD_DIM, tm)
    vt_ref[0] = v_t.reshape(N_KV_HEADS, HEAD_DIM, tm)
    kit_ref[0] = kiwi_t[0:IDX_DIM]
    if sample:
        q_ref, qi_ref, kiwi_ref, av_ref = mode_refs
        av_ref[...] = vn
        kiwi_ref[...] = kiwi
        for j in range(4):
            cols = slice(j * LANES, (j + 1) * LANES)
            q_ref[:, cols] = (rope(qall[:, cols], cos, sin) * ATTN_SCALE).astype(BF16)
        for j in range(2):
            cols = slice(j * LANES, (j + 1) * LANES)
            qi_ref[:, cols] = rope(rest[:, 256 + j * LANES:256 + (j + 1) * LANES], cos, sin).astype(BF16)
    else:
        q_ref, qit_ref, wit_ref, kib_ref, kdupt_ref, vaug_ref = mode_refs
        for j in range(4):
            cols = slice(j * LANES, (j + 1) * LANES)
            q_ref[:, cols] = (rope(qall[:, cols], cos, sin) * ATTN_SCALE).astype(BF16)
        for j in range(2):
            cols = slice(j * LANES, (j + 1) * LANES)
            qit_ref[0, cols, :] = rope(rest[:, 256 + j * LANES:256 + (j + 1) * LANES], cos, sin).T.astype(BF16)
        wit_ref[0] = kiwi_t[IDX_DIM:IDX_DIM + 8]
        kib_ref[...] = kiwi.astype(BF16)
        lo = lane < HEAD_DIM
        k_tb = k_t.astype(BF16)
        for g in range(N_KV_HEADS):
            kg = k_tb[g * HEAD_DIM:(g + 1) * HEAD_DIM]
            kg2 = jnp.concatenate([kg, kg], axis=0)
            for cc in range(tm // KEY_CHUNK):
                kdupt_ref[0, g, cc] = kg2[:, cc * KEY_CHUNK:(cc + 1) * KEY_CHUNK]
        vaug_ref[:, 0:LANES] = jnp.where(lo, v, 1.0).astype(BF16)
        vaug_ref[:, LANES:2 * LANES] = jnp.where(lo, pltpu.roll(v, HEAD_DIM, 1), 1.0).astype(BF16)


def _mixer_in(x, w_in, cos, sin, cosk, sink, ag, ab, ws, bias, batch, seq, tm, sample):
    n = batch * seq
    nj = seq // tm
    row = lambda b, j: (b * nj + j, 0)
    tab = lambda b, j: (j, 0)
    const2 = lambda b, j: (0, 0)
    const3 = lambda b, j: (0, 0, 0)
    fmajor = lambda b, j: (b, 0, j)
    n_q = N_HEADS * HEAD_DIM
    n_qi = N_IDX_HEADS * IDX_DIM
    out_shape = [
        jax.ShapeDtypeStruct((n, A_WIDTH), BF16),
        jax.ShapeDtypeStruct((batch, N_KV_HEADS, HEAD_DIM, seq), F32),
        jax.ShapeDtypeStruct((batch, N_KV_HEADS, HEAD_DIM, seq), F32),
        jax.ShapeDtypeStruct((batch, IDX_DIM, seq), F32),
    ]
    out_specs = [
        pl.BlockSpec((tm, A_WIDTH), row),
        pl.BlockSpec((1, N_KV_HEADS, HEAD_DIM, tm), lambda b, j: (b, 0, 0, j)),
        pl.BlockSpec((1, N_KV_HEADS, HEAD_DIM, tm), lambda b, j: (b, 0, 0, j)),
        pl.BlockSpec((1, IDX_DIM, tm), fmajor),
    ]
    if sample:
        out_shape += [
            jax.ShapeDtypeStruct((n, n_q), BF16),
            jax.ShapeDtypeStruct((n, n_qi), BF16),
            jax.ShapeDtypeStruct((n, LANES), F32),
            jax.ShapeDtypeStruct((n, A_WIDTH), F32),
        ]
        out_specs += [
            pl.BlockSpec((tm, n_q), row),
            pl.BlockSpec((tm, n_qi), row),
            pl.BlockSpec((tm, LANES), row),
            pl.BlockSpec((tm, A_WIDTH), row),
        ]
    else:
        out_shape += [
            jax.ShapeDtypeStruct((n, n_q), BF16),
            jax.ShapeDtypeStruct((batch, n_qi, seq), BF16),
            jax.ShapeDtypeStruct((batch, 8, seq), F32),
            jax.ShapeDtypeStruct((n, LANES), BF16),
            jax.ShapeDtypeStruct((batch, N_KV_HEADS, seq // KEY_CHUNK, LANES, KEY_CHUNK), BF16),
            jax.ShapeDtypeStruct((n, N_KV_HEADS * LANES), BF16),
        ]
        out_specs += [
            pl.BlockSpec((tm, n_q), row),
            pl.BlockSpec((1, n_qi, tm), fmajor),
            pl.BlockSpec((1, 8, tm), fmajor),
            pl.BlockSpec((tm, LANES), row),
            pl.BlockSpec((1, N_KV_HEADS, tm // KEY_CHUNK, LANES, KEY_CHUNK), lambda b, j: (b, 0, j, 0, 0)),
            pl.BlockSpec((tm, N_KV_HEADS * LANES), row),
        ]
    return pl.pallas_call(
        functools.partial(_mixer_in_kernel, sample=sample),
        grid=(batch, nj),
        in_specs=[
            pl.BlockSpec((tm, D_MODEL), row),
            pl.BlockSpec((D_MODEL, IN_PAD), const2),
            pl.BlockSpec((tm, LANES), tab),
            pl.BlockSpec((tm, LANES), tab),
            pl.BlockSpec((tm, LANES), tab),
            pl.BlockSpec((tm, LANES), tab),
            pl.BlockSpec((1, A_WIDTH), const2),
            pl.BlockSpec((1, A_WIDTH), const2),
            pl.BlockSpec((A_GROUPS, CHUNK, CHUNK), const3),
            pl.BlockSpec((CHUNK, A_WIDTH), const2),
        ],
        out_specs=out_specs,
        out_shape=out_shape,
        compiler_params=pltpu.CompilerParams(
            dimension_semantics=("arbitrary", "arbitrary"), vmem_limit_bytes=VMEM_LIMIT),
        name="mixer_in",
    )(x, w_in, cos, sin, cosk, sink, ag, ab, ws, bias)


def _sortable_key(s):
    s = jnp.where(s == 0.0, 0.0, s)
    bits = pltpu.bitcast(s, jnp.int32)
    return jnp.where(bits < 0, bits ^ jnp.int32(0x7FFFFFFF), bits)


def _kth_largest_key(count_ge, rows, k):
    kf = jnp.float32(k)
    zero = jnp.zeros((rows, 1), jnp.int32)
    cur = jnp.where(count_ge(zero) >= kf, zero, jnp.int32(INT_MIN))

    def body(it, cur):
        cand = cur | jnp.left_shift(jnp.int32(1), jnp.int32(30) - it)
        return jnp.where(count_ge(cand) >= kf, cand, cur)

    return lax.fori_loop(0, 31, body, cur)


def _upper_tri_ones():
    r = lax.broadcasted_iota(jnp.int32, (LANES, LANES), 0)
    c = lax.broadcasted_iota(jnp.int32, (LANES, LANES), 1)
    return jnp.where(r <= c, 1.0, 0.0).astype(BF16)


def _tie_break_bias(key_ref, bias_ref, tau, need, n_cols):
    tri = _upper_tri_ones()
    rows = tau.shape[0]
    run = jnp.zeros((rows, 1), F32)
    for c in range(n_cols // LANES):
        cols = slice(c * LANES, (c + 1) * LANES)
        kc = key_ref[:, cols]
        eq = jnp.where(kc == tau, 1.0, 0.0)
        rank = run + jnp.dot(eq.astype(BF16), tri, preferred_element_type=F32)
        keep = (kc > tau) | ((kc == tau) & (rank <= need))
        bias_ref[:, cols] = jnp.where(keep, 0.0, NEG_INF)
        run = run + jnp.sum(eq, axis=-1, keepdims=True)
    return run


INT16_MIN = -2 ** 15
PACK16 = 16
N_COUNT_ACCS = 8


def _count_rows_16(mask_of_group, n_rows, lanes):
    one = jnp.ones((PACK16, lanes), jnp.int16)
    zero = jnp.zeros((PACK16, lanes), jnp.int16)
    accs = [zero] * min(N_COUNT_ACCS, n_rows // PACK16)
    for r in range(n_rows // PACK16):
        accs[r % len(accs)] = accs[r % len(accs)] + jnp.where(mask_of_group(r), one, zero)
    parts = accs
    while len(parts) > 1:
        parts = [parts[j] + parts[j + 1] for j in range(0, len(parts), 2)]
    return jnp.sum(parts[0].astype(jnp.int32), axis=0, keepdims=True)


def _bcast16(x, lanes):
    return jnp.broadcast_to(x, (PACK16, lanes)).astype(jnp.int16)


def _kth_largest_16(ref, n_rows, k):
    lanes = ref.shape[1]

    def body(it, cur):
        cand = cur + jnp.left_shift(jnp.int32(1), jnp.int32(15) - it)
        c16 = _bcast16(cand, lanes)
        cnt = _count_rows_16(lambda r: ref[r * PACK16:(r + 1) * PACK16, :] >= c16, n_rows, lanes)
        return jnp.where(cnt >= k, cand, cur)

    return lax.fori_loop(0, 16, body, jnp.full((1, lanes), INT16_MIN, jnp.int32))


def _kth_largest_key_rows(hi_ref, lo_ref, n_rows, k):
    lanes = hi_ref.shape[1]
    tau_hi = _kth_largest_16(hi_ref, n_rows, jnp.int32(k))
    t16 = _bcast16(tau_hi, lanes)
    n_above = _count_rows_16(lambda r: hi_ref[r * PACK16:(r + 1) * PACK16, :] > t16, n_rows, lanes)
    min16 = jnp.full((PACK16, lanes), INT16_MIN, jnp.int16)
    for r in range(n_rows // PACK16):
        rows = slice(r * PACK16, (r + 1) * PACK16)
        lo_ref[rows, :] = jnp.where(hi_ref[rows, :] == t16, lo_ref[rows, :], min16)
    tau_lo = _kth_largest_16(lo_ref, n_rows, jnp.int32(k) - n_above)
    return jnp.left_shift(tau_hi, 16) | (tau_lo - INT16_MIN)


def _lower_tri_ones():
    r = lax.broadcasted_iota(jnp.int32, (LANES, LANES), 0)
    c = lax.broadcasted_iota(jnp.int32, (LANES, LANES), 1)
    return jnp.where(c <= r, 1.0, 0.0).astype(BF16)


def _chunk_rows(c):
    return pl.ds(pl.multiple_of(c * KEY_CHUNK, KEY_CHUNK), KEY_CHUNK)


def _select_topk_bias(i, n_chunks, qit_ref, wit_ref, kib_ref, key_ref, hi_ref, lo_ref, tau_ref, biasq_ref):
    tq = Q_BLOCK
    zeros64 = jnp.zeros((HEAD_DIM, tq), BF16)
    qi_rhs = jnp.concatenate(
        [jnp.concatenate([qit_ref[0, h * IDX_DIM:(h + 1) * IDX_DIM, :], zeros64], axis=0)
         for h in range(N_IDX_HEADS)], axis=1)
    wi = wit_ref[0]
    q_pos = i * tq + lax.broadcasted_iota(jnp.int32, (KEY_CHUNK, tq), 1)
    row_iota = lax.broadcasted_iota(jnp.int32, (KEY_CHUNK, tq), 0)

    def keys_of(c, carry):
        dots = jnp.dot(kib_ref[_chunk_rows(c), :], qi_rhs, preferred_element_type=F32)
        s = jnp.zeros((KEY_CHUNK, tq), F32)
        for h in range(N_IDX_HEADS):
            s = s + wi[h:h + 1, :] * jnp.maximum(dots[:, h * tq:(h + 1) * tq], 0.0)
        visible = (c * KEY_CHUNK + row_iota) <= q_pos
        key = _sortable_key(jnp.where(visible, s, NEG_INF))
        rows = _chunk_rows(c)
        key_ref[rows, :] = key
        hi_ref[rows, :] = (key >> 16).astype(jnp.int16)
        lo_ref[rows, :] = ((key & 0xFFFF) + INT16_MIN).astype(jnp.int16)
        return carry

    lax.fori_loop(0, n_chunks, keys_of, 0)

    for c in range(1, key_ref.shape[0] // KEY_CHUNK):
        @pl.when(i == c)
        def _(c=c):
            tau_c = _kth_largest_key_rows(hi_ref, lo_ref, (c + 1) * KEY_CHUNK, TOPK_MAX)
            tau_ref[...] = jnp.broadcast_to(tau_c, tau_ref.shape)

    tau = tau_ref[0:1, :]

    def bias_of(c, counts):
        n_ge, n_gt = counts
        key = key_ref[_chunk_rows(c), :]
        ge = key >= tau
        biasq_ref[c] = jnp.where(ge, 0.0, NEG_INF).T
        n_ge = n_ge + jnp.sum(jnp.where(ge, 1.0, 0.0), axis=0, keepdims=True)
        n_gt = n_gt + jnp.sum(jnp.where(key > tau, 1.0, 0.0), axis=0, keepdims=True)
        return n_ge, n_gt

    zero = jnp.zeros((1, tq), F32)
    n_ge, n_gt = lax.fori_loop(0, n_chunks, bias_of, (zero, zero))

    @pl.when(jnp.max(n_ge) > jnp.float32(TOPK_MAX))
    def _():
        need = jnp.float32(TOPK_MAX) - n_gt
        tri = _lower_tri_ones()

        def tie_of(c, run):
            halves = []
            for half in range(KEY_CHUNK // LANES):
                rows = pl.ds(pl.multiple_of(c * KEY_CHUNK + half * LANES, LANES), LANES)
                kc = key_ref[rows, :]
                eq = jnp.where(kc == tau, 1.0, 0.0)
                rank = run + jnp.dot(tri, eq.astype(BF16), preferred_element_type=F32)
                keep = (kc > tau) | ((kc == tau) & (rank <= need))
                halves.append(jnp.where(keep, 0.0, NEG_INF))
                run = run + jnp.sum(eq, axis=0, keepdims=True)
            biasq_ref[c] = jnp.concatenate(halves, axis=0).T
            return run

        lax.fori_loop(0, n_chunks, tie_of, zero)


def _prompt_attn_kernel(q_ref, qit_ref, wit_ref, kib_ref, kdupt_ref, vaug_ref, o_ref,
                        key_ref, hi_ref, lo_ref, tau_ref, biasq_ref, lg_ref, m_ref, acc_ref):
    i = pl.program_id(1)
    n_chunks = i + 1
    tq = Q_BLOCK

    @pl.when(i == 0)
    def _():
        k_pos = lax.broadcasted_iota(jnp.int32, (tq, KEY_CHUNK), 1)
        q_pos = lax.broadcasted_iota(jnp.int32, (tq, KEY_CHUNK), 0)
        biasq_ref[0] = jnp.where(k_pos <= q_pos, 0.0, NEG_INF)

    @pl.when(i > 0)
    def _():
        _select_topk_bias(i, n_chunks, qit_ref, wit_ref, kib_ref, key_ref, hi_ref, lo_ref, tau_ref, biasq_ref)

    lane = lax.broadcasted_iota(jnp.int32, (tq, LANES), 1)
    lo = lane < HEAD_DIM
    q4 = []
    for g in range(N_KV_HEADS):
        parts = []
        for j in range(GROUP_SIZE // 2):
            blk = q_ref[:, g * 2 * LANES + j * LANES:g * 2 * LANES + (j + 1) * LANES].astype(F32)
            parts.append(jnp.where(lo, blk, 0.0).astype(BF16))
            parts.append(jnp.where(lo, 0.0, blk).astype(BF16))
        q4.append(jnp.concatenate(parts, axis=0))
    m_ref[...] = jnp.full(m_ref.shape, NEG_INF, F32)

    def mask_and_max(c, carry):
        bias = biasq_ref[c]
        for g in range(N_KV_HEADS):
            raw = jnp.dot(q4[g], kdupt_ref[0, g, c], preferred_element_type=F32)
            for r in range(GROUP_SIZE):
                rows = slice(r * tq, (r + 1) * tq)
                lg = raw[rows] + bias
                lg_ref[g, c, rows, :] = lg
                m_ref[g, rows, :] = jnp.maximum(m_ref[g, rows, :], jnp.maximum(lg[:, :LANES], lg[:, LANES:]))
        return carry

    lax.fori_loop(0, n_chunks, mask_and_max, 0)
    m_ref[...] = jnp.broadcast_to(jnp.max(m_ref[...], axis=-1, keepdims=True), m_ref.shape)
    acc_ref[...] = jnp.zeros(acc_ref.shape, F32)

    def accumulate(c, carry):
        for g in range(N_KV_HEADS):
            m = m_ref[g]
            lg = lg_ref[g, c]
            p = jnp.concatenate([jnp.exp(lg[:, :LANES] - m), jnp.exp(lg[:, LANES:] - m)], axis=1).astype(BF16)
            acc_ref[g] += jnp.dot(p, vaug_ref[_chunk_rows(c), g * LANES:(g + 1) * LANES],
                                  preferred_element_type=F32)
        return carry

    lax.fori_loop(0, n_chunks, accumulate, 0)
    for g in range(N_KV_HEADS):
        acc = acc_ref[g]
        out = acc * (1.0 / pltpu.roll(acc, HEAD_DIM, 1))
        for j in range(GROUP_SIZE // 2):
            even = out[(2 * j) * tq:(2 * j + 1) * tq]
            odd = pltpu.roll(out[(2 * j + 1) * tq:(2 * j + 2) * tq], HEAD_DIM, 1)
            o_ref[:, g * 2 * LANES + j * LANES:g * 2 * LANES + (j + 1) * LANES] = jnp.where(
                lo, even, odd).astype(BF16)


def _prompt_attention(q, qit, wit, kib, kdupt, vaug, batch, seq):
    nb = seq // Q_BLOCK
    n_q = N_HEADS * HEAD_DIM
    n_qi = N_IDX_HEADS * IDX_DIM
    n_kc = seq // KEY_CHUNK
    qblk = lambda b, i: (b, 0, i)
    perb = lambda b, i: (b, 0)
    return pl.pallas_call(
        _prompt_attn_kernel,
        grid=(batch, nb),
        in_specs=[
            pl.BlockSpec((Q_BLOCK, n_q), lambda b, i: (b * nb + i, 0)),
            pl.BlockSpec((1, n_qi, Q_BLOCK), qblk),
            pl.BlockSpec((1, 8, Q_BLOCK), qblk),
            pl.BlockSpec((seq, LANES), perb),
            pl.BlockSpec((1, N_KV_HEADS, n_kc, LANES, KEY_CHUNK), lambda b, i: (b, 0, 0, 0, 0)),
            pl.BlockSpec((seq, N_KV_HEADS * LANES), perb),
        ],
        out_specs=pl.BlockSpec((Q_BLOCK, n_q), lambda b, i: (b * nb + i, 0)),
        out_shape=jax.ShapeDtypeStruct((batch * seq, n_q), BF16),
        scratch_shapes=[
            pltpu.VMEM((seq, Q_BLOCK), jnp.int32),
            pltpu.VMEM((seq, Q_BLOCK), jnp.int16),
            pltpu.VMEM((seq, Q_BLOCK), jnp.int16),
            pltpu.VMEM((8, Q_BLOCK), jnp.int32),
            pltpu.VMEM((n_kc, Q_BLOCK, KEY_CHUNK), F32),
            pltpu.VMEM((N_KV_HEADS, n_kc, GROUP_SIZE * Q_BLOCK, KEY_CHUNK), F32),
            pltpu.VMEM((N_KV_HEADS, GROUP_SIZE * Q_BLOCK, LANES), F32),
            pltpu.VMEM((N_KV_HEADS, GROUP_SIZE * Q_BLOCK, LANES), F32),
        ],
        compiler_params=pltpu.CompilerParams(
            dimension_semantics=("arbitrary", "arbitrary"), vmem_limit_bytes=VMEM_LIMIT),
        name="prompt_attention",
    )(q, qit, wit, kib, kdupt, vaug)


def _page_copies(pt_ref, b, n_pages, src_hbm, dst_for_page, sem):
    return [pltpu.make_async_copy(src_hbm.at[pt_ref[b * n_pages + p]], dst_for_page(p), sem)
            for p in range(n_pages)]


def _sample_scores_kernel(pt_ref, a_ref, w_ref, kidx_hbm, o_ref, buf, sem, *, n_pages):
    b = pl.program_id(0)
    nb = pl.num_programs(0)
    slot = b % 2

    def copies(bb, sl):
        return _page_copies(pt_ref, bb, n_pages, kidx_hbm,
                            lambda p: buf.at[sl, :, pl.ds(p * PAGE_SIZE, PAGE_SIZE)], sem.at[sl])

    @pl.when(b == 0)
    def _():
        for cp in copies(b, slot):
            cp.start()

    @pl.when(b + 1 < nb)
    def _():
        for cp in copies(b + 1, 1 - slot):
            cp.start()

    for cp in copies(b, slot):
        cp.wait()
    dots = jnp.dot(a_ref[0], buf[slot].astype(BF16), preferred_element_type=F32)
    o_ref[0] = jnp.sum(jnp.maximum(dots, 0.0) * w_ref[0], axis=0, keepdims=True)


def _sample_scores(pt_flat, a, w, kidx_t, n_samples, n_pages):
    past = n_pages * PAGE_SIZE
    return pl.pallas_call(
        functools.partial(_sample_scores_kernel, n_pages=n_pages),
        grid_spec=pltpu.PrefetchScalarGridSpec(
            num_scalar_prefetch=1,
            grid=(n_samples,),
            in_specs=[
                pl.BlockSpec((1, 16, IDX_DIM), lambda b, pt: (b, 0, 0)),
                pl.BlockSpec((1, 16, 1), lambda b, pt: (b, 0, 0)),
                pl.BlockSpec(memory_space=pl.ANY),
            ],
            out_specs=pl.BlockSpec((1, 1, past), lambda b, pt: (b, 0, 0)),
            scratch_shapes=[pltpu.VMEM((2, IDX_DIM, past), F32), pltpu.SemaphoreType.DMA((2,))],
        ),
        out_shape=jax.ShapeDtypeStruct((n_samples, 1, past), F32),
        compiler_params=pltpu.CompilerParams(
            dimension_semantics=("arbitrary",), vmem_limit_bytes=VMEM_LIMIT),
        name="sample_scores",
    )(pt_flat, a, w, kidx_t)


def _sample_select_kernel(s_ref, qi_ref, kiwi_ref, bias_ref, bias_new_ref, key_ref, *, topk):
    rows, past = s_ref.shape
    lane = lax.broadcasted_iota(jnp.int32, (rows, LANES), 1)
    lo = lane < IDX_DIM
    kiwi = kiwi_ref[...]
    kib = kiwi.astype(BF16).astype(F32)
    kidup = jnp.where(lo, kib, pltpu.roll(kib, IDX_DIM, 1))
    s_new = jnp.zeros((rows, 1), F32)
    for j in range(2):
        t = qi_ref[:, j * LANES:(j + 1) * LANES].astype(F32) * kidup
        for half in range(2):
            d = jnp.sum(jnp.where(lo if half == 0 else jnp.logical_not(lo), t, 0.0), axis=-1, keepdims=True)
            h = 2 * j + half
            s_new = s_new + kiwi[:, IDX_DIM + h:IDX_DIM + h + 1] * jnp.maximum(d, 0.0)
    key_new = _sortable_key(s_new)
    key_ref[...] = _sortable_key(s_ref[...])

    def count_ge(c):
        return (jnp.sum(jnp.where(key_ref[...] >= c, 1.0, 0.0), axis=-1, keepdims=True)
                + jnp.where(key_new >= c, 1.0, 0.0))

    tau = _kth_largest_key(count_ge, rows, topk)
    key = key_ref[...]
    bias_ref[...] = jnp.where(key >= tau, 0.0, NEG_INF)
    n_gt = (jnp.sum(jnp.where(key > tau, 1.0, 0.0), axis=-1, keepdims=True)
            + jnp.where(key_new > tau, 1.0, 0.0))
    need = jnp.float32(topk) - n_gt
    bias_new_ref[...] = jnp.broadcast_to(jnp.where(key_new >= tau, 0.0, NEG_INF), (rows, LANES))

    @pl.when(jnp.max(count_ge(tau)) > jnp.float32(topk))
    def _():
        n_eq_past = _tie_break_bias(key_ref, bias_ref, tau, need, past)
        keep_new = (key_new > tau) | ((key_new == tau) & (n_eq_past + 1.0 <= need))
        bias_new_ref[...] = jnp.broadcast_to(jnp.where(keep_new, 0.0, NEG_INF), (rows, LANES))


def _sample_select(scores, qi, kiwi, topk):
    rows, past = scores.shape
    return pl.pallas_call(
        functools.partial(_sample_select_kernel, topk=topk),
        out_shape=[jax.ShapeDtypeStruct((rows, past), F32), jax.ShapeDtypeStruct((rows, LANES), F32)],
        scratch_shapes=[pltpu.VMEM((rows, past), jnp.int32)],
        compiler_params=pltpu.CompilerParams(vmem_limit_bytes=VMEM_LIMIT),
        name="sample_select",
    )(scores, qi, kiwi)


def _sample_attn_kernel(pt_ref, aq_ref, knew_ref, vnew_ref, bias_ref, bias_new_ref, k_hbm, v_hbm, o_ref,
                        kbuf, vbuf, sem, *, n_pages):
    b = pl.program_id(0)
    nb = pl.num_programs(0)
    slot = b % 2

    def copies(bb, sl):
        dst_k = lambda p: kbuf.at[sl, :, :, pl.ds(p * PAGE_SIZE, PAGE_SIZE)]
        dst_v = lambda p: vbuf.at[sl, :, :, pl.ds(p * PAGE_SIZE, PAGE_SIZE)]
        return (_page_copies(pt_ref, bb, n_pages, k_hbm, dst_k, sem.at[0, sl])
                + _page_copies(pt_ref, bb, n_pages, v_hbm, dst_v, sem.at[1, sl]))

    @pl.when(b == 0)
    def _():
        for cp in copies(b, slot):
            cp.start()

    @pl.when(b + 1 < nb)
    def _():
        for cp in copies(b + 1, 1 - slot):
            cp.start()

    for cp in copies(b, slot):
        cp.wait()
    bias = bias_ref[0]
    bias_new = bias_new_ref[0][:, 0:1]
    for g in range(N_KV_HEADS):
        aq = aq_ref[0, g]
        logits = jnp.dot(aq, kbuf[slot, g].astype(BF16), preferred_element_type=F32) + bias
        knew = knew_ref[0, g].astype(BF16).astype(F32)
        l_new = jnp.sum(aq.astype(F32) * knew, axis=-1, keepdims=True) + bias_new
        m = jnp.maximum(jnp.max(logits, axis=-1, keepdims=True), l_new)
        p = jnp.exp(logits - m)
        p_new = jnp.exp(l_new - m)
        l = jnp.sum(p, axis=-1, keepdims=True) + p_new
        pv = lax.dot_general(p.astype(BF16), vbuf[slot, g].astype(BF16), (((1,), (1,)), ((), ())),
                             preferred_element_type=F32)
        pv = pv + p_new.astype(BF16).astype(F32) * vnew_ref[0, g].astype(BF16).astype(F32)
        o_ref[0, g] = pv * (1.0 / l)


def _sample_attention(pt_flat, aq, knew, vnew, bias, bias_new, k_t, v_t, n_samples, n_pages):
    past = n_pages * PAGE_SIZE
    blk4 = lambda b, pt: (b, 0, 0, 0)
    blk3 = lambda b, pt: (b, 0, 0)
    return pl.pallas_call(
        functools.partial(_sample_attn_kernel, n_pages=n_pages),
        grid_spec=pltpu.PrefetchScalarGridSpec(
            num_scalar_prefetch=1,
            grid=(n_samples,),
            in_specs=[
                pl.BlockSpec((1, N_KV_HEADS, 16, HEAD_DIM), blk4),
                pl.BlockSpec((1, N_KV_HEADS, 1, HEAD_DIM), blk4),
                pl.BlockSpec((1, N_KV_HEADS, 1, HEAD_DIM), blk4),
                pl.BlockSpec((1, 1, past), blk3),
                pl.BlockSpec((1, 1, LANES), blk3),
                pl.BlockSpec(memory_space=pl.ANY),
                pl.BlockSpec(memory_space=pl.ANY),
            ],
            out_specs=pl.BlockSpec((1, N_KV_HEADS, 16, HEAD_DIM), blk4),
            scratch_shapes=[
                pltpu.VMEM((2, N_KV_HEADS, HEAD_DIM, past), F32),
                pltpu.VMEM((2, N_KV_HEADS, HEAD_DIM, past), F32),
                pltpu.SemaphoreType.DMA((2, 2)),
            ],
        ),
        out_shape=jax.ShapeDtypeStruct((n_samples, N_KV_HEADS, 16, HEAD_DIM), F32),
        compiler_params=pltpu.CompilerParams(
            dimension_semantics=("arbitrary",), vmem_limit_bytes=VMEM_LIMIT),
        name="sample_attention",
    )(pt_flat, aq, knew, vnew, bias, bias_new, k_t, v_t)


def _rope_tables(pos):
    half = HEAD_DIM // 2
    inv = ROPE_THETA ** (-jnp.arange(half, dtype=F32) / half)
    ang = pos.astype(F32)[:, None] * inv[None, :]
    c = jnp.cos(ang)
    s = jnp.sin(ang)
    cos64 = jnp.concatenate([c, c], axis=1)
    sin64 = jnp.concatenate([-s, s], axis=1)
    n = pos.shape[0]
    cos = jnp.concatenate([cos64, cos64], axis=1)
    sin = jnp.concatenate([sin64, sin64], axis=1)
    cosk = jnp.concatenate([cos64, jnp.full((n, LANES - IDX_DIM), IDX_W_SCALE, F32)], axis=1)
    sink = jnp.concatenate([sin64, jnp.zeros((n, LANES - IDX_DIM), F32)], axis=1)
    return cos, sin, cosk, sink


def kernel(x_prompt, x_sample, cache_k, cache_v, cache_kidx, page_table, ln1_g, ln1_b, ffn1_w_up, ffn1_w_down, ln2_g, ln2_b, w_in, a_ln_g, a_ln_b, a_ws, a_bs, w_out, ln3_g, ln3_b, ffn2_w_up, ffn2_w_down):
    batch, seq, _ = x_prompt.shape
    n_samples = x_sample.shape[0]
    n_pages = page_table.shape[1]
    past = n_pages * PAGE_SIZE
    l = 0

    wgu1, wd1 = ffn1_w_up[l].astype(BF16), ffn1_w_down[l].astype(BF16)
    wgu2, wd2 = ffn2_w_up[l].astype(BF16), ffn2_w_down[l].astype(BF16)
    w_in_p = jnp.pad(w_in[l], ((0, 0), (0, IN_PAD - IN_WIDTH))).astype(BF16)
    w_out_p = w_out[l].reshape(2, A_WIDTH, D_MODEL).astype(BF16)
    causal = jnp.tril(jnp.ones((CHUNK, CHUNK), dtype=bool))
    ws_prompt = jnp.where(causal[None], a_ws[l], 0).astype(BF16)
    bias_prompt = jnp.repeat(a_bs[l].T, A_WIDTH // A_GROUPS, axis=1)
    eye = jnp.eye(CHUNK, dtype=F32)
    ws_sample = (a_ws[l][:, 0, 0][:, None, None] * eye[None]).astype(BF16)
    bias_sample = jnp.broadcast_to(bias_prompt[0:1], (CHUNK, A_WIDTH))

    pos_p = jnp.arange(seq, dtype=jnp.int32)
    pos_s = jnp.full((n_samples,), past, dtype=jnp.int32)
    tabs_p = _rope_tables(pos_p)
    tabs_s = _rope_tables(pos_s)

    xp = x_prompt.reshape(batch * seq, D_MODEL)
    x1p = _ffn_ln(xp, wgu1, wd1, ln1_g, ln1_b, tm=512)
    a_p, kt_p, vt_p, kit_p, q_p, qit_p, wit_p, kib_p, kdupt_p, vaug_p = _mixer_in(
        x1p, w_in_p, *tabs_p, a_ln_g, a_ln_b, ws_prompt, bias_prompt, batch, seq, tm=512, sample=False)
    b_p = _prompt_attention(q_p, qit_p, wit_p, kib_p, kdupt_p, vaug_p, batch, seq)
    yp = _mix_ffn_ln(a_p, b_p, x1p, w_out_p, ln2_g, ln2_b, wgu2, wd2, ln3_g, ln3_b, tm=512)

    xs = x_sample.reshape(n_samples, D_MODEL)
    x1s = _ffn_ln(xs, wgu1, wd1, ln1_g, ln1_b, tm=n_samples)
    a_s, kt_s, vt_s, kit_s, q_s, qi_s, kiwi_s, av_s = _mixer_in(
        x1s, w_in_p, *tabs_s, a_ln_g, a_ln_b, ws_sample, bias_sample, 1, n_samples, tm=n_samples, sample=True)
    pt_flat = page_table.reshape(-1)
    kidx_t = jnp.transpose(cache_kidx[l], (0, 2, 1))
    k_t = jnp.transpose(cache_k[l], (0, 2, 3, 1))
    v_t = jnp.transpose(cache_v[l], (0, 2, 3, 1))
    a_idx = jnp.pad(qi_s.reshape(n_samples, N_IDX_HEADS, IDX_DIM), ((0, 0), (0, 16 - N_IDX_HEADS), (0, 0)))
    w_idx = jnp.pad(kiwi_s[:, IDX_DIM:IDX_DIM + N_IDX_HEADS], ((0, 0), (0, 16 - N_IDX_HEADS)))[..., None]
    scores = _sample_scores(pt_flat, a_idx, w_idx, kidx_t, n_samples, n_pages)
    topk = min(TOPK_MAX, (past + 1) // 4)
    bias_s, bias_new = _sample_select(scores.reshape(n_samples, past), qi_s, kiwi_s, topk)
    aq = jnp.pad(q_s.reshape(n_samples, N_KV_HEADS, GROUP_SIZE, HEAD_DIM),
                 ((0, 0), (0, 0), (0, 16 - GROUP_SIZE), (0, 0)))
    knew = jnp.transpose(kt_s[0], (2, 0, 1))[:, :, None, :]
    vnew = jnp.transpose(vt_s[0], (2, 0, 1))[:, :, None, :]
    o_s = _sample_attention(pt_flat, aq, knew, vnew, bias_s.reshape(n_samples, 1, past),
                            bias_new.reshape(n_samples, 1, LANES), k_t, v_t, n_samples, n_pages)
    b_s = o_s[:, :, :GROUP_SIZE, :].reshape(n_samples, N_HEADS * HEAD_DIM).astype(BF16)
    ys = _mix_ffn_ln(a_s, b_s, x1s, w_out_p, ln2_g, ln2_b, wgu2, wd2, ln3_g, ln3_b, tm=n_samples)

    new_k_p = jnp.transpose(kt_p, (0, 3, 1, 2))[None]
    new_v_p = jnp.transpose(vt_p, (0, 3, 1, 2))[None]
    new_ki_p = jnp.transpose(kit_p, (0, 2, 1))[None]
    new_k_s = jnp.transpose(kt_s[0], (2, 0, 1))[None, :, None]
    new_v_s = jnp.transpose(vt_s[0], (2, 0, 1))[None, :, None]
    new_ki_s = jnp.transpose(kit_s[0], (1, 0))[None, :, None]
    return (yp.reshape(batch, seq, D_MODEL), ys.reshape(n_samples, 1, D_MODEL),
            new_k_p, new_v_p, new_ki_p, new_k_s, new_v_s, new_ki_s,
            av_s.reshape(1, n_samples, 1, A_WIDTH))
```

```python
import functools

import jax
import jax.numpy as jnp
from jax import lax
from jax.experimental import pallas as pl
from jax.experimental.pallas import tpu as pltpu

F32 = jnp.float32
BF16 = jnp.bfloat16

D_MODEL = 1024
D_FF = 2816
A_WIDTH = 512
A_GROUPS = 4
CHUNK = 128
HEAD_DIM = 64
N_HEADS = 8
N_KV_HEADS = 2
GROUP_SIZE = N_HEADS // N_KV_HEADS
N_IDX_HEADS = 4
IDX_DIM = 64
IDX_W_SCALE = (N_IDX_HEADS * IDX_DIM) ** -0.5
TOPK_MAX = 256
PAGE_SIZE = 128
ROPE_THETA = 10000.0
LN_EPS = 1e-5
DEPTH = 1
ALPHA = (2.0 * DEPTH) ** 0.25
ATTN_SCALE = HEAD_DIM ** -0.5

IN_SPLITS = (512, 512, 512, 128, 128, 256, 64, 4)
IN_WIDTH = sum(IN_SPLITS)
LANES = 128
IN_PAD = -(-IN_WIDTH // LANES) * LANES
COL_Q = 1024
COL_K = 1536
COL_V = 1664
COL_QI = 1792
COL_KIWI = 2048

Q_BLOCK = 256
KEY_CHUNK = 256
FF_CHUNK = 256
N_FF_CHUNKS = D_FF // FF_CHUNK
VMEM_LIMIT = 56 * 1024 * 1024
INT_MIN = -2 ** 31
NEG_INF = float("-inf")


def _layernorm(y, g, b):
    mu = jnp.mean(y, axis=-1, keepdims=True)
    d = y - mu
    var = jnp.mean(d * d, axis=-1, keepdims=True)
    return d * lax.rsqrt(var + LN_EPS) * g + b


def _gelu_tanh(x):
    return 0.5 * x * (1.0 + jnp.tanh(0.7978845608028654 * (x + 0.044715 * (x * x * x))))


def _swiglu_ln(x, wup_ref, wd_ref, g_ref, b_ref):
    xb = x.astype(BF16)
    acc = jnp.zeros(x.shape, F32)
    for c in range(N_FF_CHUNKS):
        cols = slice(c * FF_CHUNK, (c + 1) * FF_CHUNK)
        gate = jnp.dot(xb, wup_ref[:, cols], preferred_element_type=F32)
        up = jnp.dot(xb, wup_ref[:, D_FF + c * FF_CHUNK:D_FF + (c + 1) * FF_CHUNK], preferred_element_type=F32)
        h = (gate * (1.0 / (1.0 + jnp.exp(-gate))) * up).astype(BF16)
        acc = acc + jnp.dot(h, wd_ref[cols, :], preferred_element_type=F32)
    return _layernorm(ALPHA * x + 0.5 * acc, g_ref[...], b_ref[...])


def _ffn_ln_kernel(x_ref, wgu_ref, wd_ref, g_ref, b_ref, o_ref):
    o_ref[...] = _swiglu_ln(x_ref[...], wgu_ref, wd_ref, g_ref, b_ref)


def _mix_ffn_ln_kernel(a_ref, b_ref, x_ref, wo_ref, g2_ref, b2_ref, wgu_ref, wd_ref, g3_ref, b3_ref, o_ref):
    mix = (jnp.dot(a_ref[...], wo_ref[0], preferred_element_type=F32)
           + jnp.dot(b_ref[...], wo_ref[1], preferred_element_type=F32))
    x2 = _layernorm(ALPHA * x_ref[...] + mix, g2_ref[...], b2_ref[...])
    o_ref[...] = _swiglu_ln(x2, wgu_ref, wd_ref, g3_ref, b3_ref)


def _resident(shape):
    return pl.BlockSpec(shape, lambda i: (0,) * len(shape), pipeline_mode=pl.Buffered(1))


def _ffn_ln(x, wgu, wd, g, b, tm):
    n = x.shape[0]
    row = lambda i: (i, 0)
    return pl.pallas_call(
        _ffn_ln_kernel,
        grid=(n // tm,),
        in_specs=[
            pl.BlockSpec((tm, D_MODEL), row),
            _resident((D_MODEL, 2 * D_FF)),
            _resident((D_FF, D_MODEL)),
            _resident((1, D_MODEL)),
            _resident((1, D_MODEL)),
        ],
        out_specs=pl.BlockSpec((tm, D_MODEL), row),
        out_shape=jax.ShapeDtypeStruct((n, D_MODEL), F32),
        compiler_params=pltpu.CompilerParams(
            dimension_semantics=("arbitrary",), vmem_limit_bytes=VMEM_LIMIT),
        name="ffn_ln",
    )(x, wgu, wd, g, b)


def _mix_ffn_ln(a, b, x, wo, g2, b2, wgu, wd, g3, b3, tm):
    n = x.shape[0]
    row = lambda i: (i, 0)
    return pl.pallas_call(
        _mix_ffn_ln_kernel,
        grid=(n // tm,),
        in_specs=[
            pl.BlockSpec((tm, A_WIDTH), row),
            pl.BlockSpec((tm, N_HEADS * HEAD_DIM), row),
            pl.BlockSpec((tm, D_MODEL), row),
            _resident((2, A_WIDTH, D_MODEL)),
            _resident((1, D_MODEL)),
            _resident((1, D_MODEL)),
            _resident((D_MODEL, 2 * D_FF)),
            _resident((D_FF, D_MODEL)),
            _resident((1, D_MODEL)),
            _resident((1, D_MODEL)),
        ],
        out_specs=pl.BlockSpec((tm, D_MODEL), row),
        out_shape=jax.ShapeDtypeStruct((n, D_MODEL), F32),
        compiler_params=pltpu.CompilerParams(
            dimension_semantics=("arbitrary",), vmem_limit_bytes=VMEM_LIMIT),
        name="mix_ffn_ln",
    )(a, b, x, wo, g2, b2, wgu, wd, g3, b3)


def _mixer_in_kernel(x_ref, w_ref, cos_ref, sin_ref, cosk_ref, sink_ref, ag_ref, ab_ref, ws_ref,
                     bias_ref, aout_ref, kt_ref, vt_ref, kit_ref, *mode_refs, sample):
    tm = x_ref.shape[0]
    xb = x_ref[...].astype(BF16)

    def proj(lo, hi):
        return jnp.dot(xb, w_ref[:, lo:hi], preferred_element_type=F32)

    u = _gelu_tanh(proj(0, A_WIDTH))
    vn = _layernorm(_gelu_tanh(proj(A_WIDTH, 2 * A_WIDTH)), ag_ref[...], ab_ref[...])
    vnb = vn.astype(BF16)
    for c in range(tm // CHUNK):
        rows = slice(c * CHUNK, (c + 1) * CHUNK)
        for g in range(A_GROUPS):
            cols = slice(g * LANES, (g + 1) * LANES)
            mixed = jnp.dot(ws_ref[g], vnb[rows, cols], preferred_element_type=F32) + bias_ref[:, cols]
            aout_ref[rows, cols] = (u[rows, cols] * mixed).astype(BF16)

    lane = lax.broadcasted_iota(jnp.int32, (tm, LANES), 1)
    first_half = (lane & (HEAD_DIM - 1)) < (HEAD_DIM // 2)

    def rope(x, cos, sin):
        partner = jnp.where(first_half, pltpu.roll(x, LANES - HEAD_DIM // 2, 1),
                            pltpu.roll(x, HEAD_DIM // 2, 1))
        return x * cos + partner * sin

    cos = cos_ref[...]
    sin = sin_ref[...]
    qall = proj(COL_Q, COL_K)
    rest = proj(COL_K, IN_PAD)
    k = rope(rest[:, 0:128], cos, sin)
    v = rest[:, 128:256]
    kiwi = rope(rest[:, 512:640], cosk_ref[...], sink_ref[...])
    k_t = k.T
    v_t = v.T
    kiwi_t = kiwi.T
    kt_ref[0] = k_t.reshape(N_KV_HEADS, HEAD_DIM, tm)
    vt_ref[0] = v_t.reshape(N_KV_HEADS, HEAD_DIM, tm)
    kit_ref[0] = kiwi_t[0:IDX_DIM]
    if sample:
        q_ref, qi_ref, kiwi_ref, av_ref = mode_refs
        av_ref[...] = vn
        kiwi_ref[...] = kiwi
        for j in range(4):
            cols = slice(j * LANES, (j + 1) * LANES)
            q_ref[:, cols] = (rope(qall[:, cols], cos, sin) * ATTN_SCALE).astype(BF16)
        for j in range(2):
            cols = slice(j * LANES, (j + 1) * LANES)
            qi_ref[:, cols] = rope(rest[:, 256 + j * LANES:256 + (j + 1) * LANES], cos, sin).astype(BF16)
    else:
        q_ref, qit_ref, wit_ref, kib_ref, kdupt_ref, vaug_ref = mode_refs
        for j in range(4):
            cols = slice(j * LANES, (j + 1) * LANES)
            q_ref[:, cols] = (rope(qall[:, cols], cos, sin) * ATTN_SCALE).astype(BF16)
        for j in range(2):
            cols = slice(j * LANES, (j + 1) * LANES)
            qit_ref[0, cols, :] = rope(rest[:, 256 + j * LANES:256 + (j + 1) * LANES], cos, sin).T.astype(BF16)
        wit_ref[0] = kiwi_t[IDX_DIM:IDX_DIM + 8]
        kib_ref[...] = kiwi.astype(BF16)
        lo = lane < HEAD_DIM
        k_tb = k_t.astype(BF16)
        for g in range(N_KV_HEADS):
            kg = k_tb[g * HEAD_DIM:(g + 1) * HEAD_DIM]
            kg2 = jnp.concatenate([kg, kg], axis=0)
            for cc in range(tm // KEY_CHUNK):
                kdupt_ref[0, g, cc] = kg2[:, cc * KEY_CHUNK:(cc + 1) * KEY_CHUNK]
        vaug_ref[:, 0:LANES] = jnp.where(lo, v, 1.0).astype(BF16)
        vaug_ref[:, LANES:2 * LANES] = jnp.where(lo, pltpu.roll(v, HEAD_DIM, 1), 1.0).astype(BF16)


def _mixer_in(x, w_in, cos, sin, cosk, sink, ag, ab, ws, bias, batch, seq, tm, sample):
    n = batch * seq
    nj = seq // tm
    row = lambda b, j: (b * nj + j, 0)
    tab = lambda b, j: (j, 0)
    const2 = lambda b, j: (0, 0)
    const3 = lambda b, j: (0, 0, 0)
    fmajor = lambda b, j: (b, 0, j)
    n_q = N_HEADS * HEAD_DIM
    n_qi = N_IDX_HEADS * IDX_DIM
    out_shape = [
        jax.ShapeDtypeStruct((n, A_WIDTH), BF16),
        jax.ShapeDtypeStruct((batch, N_KV_HEADS, HEAD_DIM, seq), F32),
        jax.ShapeDtypeStruct((batch, N_KV_HEADS, HEAD_DIM, seq), F32),
        jax.ShapeDtypeStruct((batch, IDX_DIM, seq), F32),
    ]
    out_specs = [
        pl.BlockSpec((tm, A_WIDTH), row),
        pl.BlockSpec((1, N_KV_HEADS, HEAD_DIM, tm), lambda b, j: (b, 0, 0, j)),
        pl.BlockSpec((1, N_KV_HEADS, HEAD_DIM, tm), lambda b, j: (b, 0, 0, j)),
        pl.BlockSpec((1, IDX_DIM, tm), fmajor),
    ]
    if sample:
        out_shape += [
            jax.ShapeDtypeStruct((n, n_q), BF16),
            jax.ShapeDtypeStruct((n, n_qi), BF16),
            jax.ShapeDtypeStruct((n, LANES), F32),
            jax.ShapeDtypeStruct((n, A_WIDTH), F32),
        ]
        out_specs += [
            pl.BlockSpec((tm, n_q), row),
            pl.BlockSpec((tm, n_qi), row),
            pl.BlockSpec((tm, LANES), row),
            pl.BlockSpec((tm, A_WIDTH), row),
        ]
    else:
        out_shape += [
            jax.ShapeDtypeStruct((n, n_q), BF16),
            jax.ShapeDtypeStruct((batch, n_qi, seq), BF16),
            jax.ShapeDtypeStruct((batch, 8, seq), F32),
            jax.ShapeDtypeStruct((n, LANES), BF16),
            jax.ShapeDtypeStruct((batch, N_KV_HEADS, seq // KEY_CHUNK, LANES, KEY_CHUNK), BF16),
            jax.ShapeDtypeStruct((n, N_KV_HEADS * LANES), BF16),
        ]
        out_specs += [
            pl.BlockSpec((tm, n_q), row),
            pl.BlockSpec((1, n_qi, tm), fmajor),
            pl.BlockSpec((1, 8, tm), fmajor),
            pl.BlockSpec((tm, LANES), row),
            pl.BlockSpec((1, N_KV_HEADS, tm // KEY_CHUNK, LANES, KEY_CHUNK), lambda b, j: (b, 0, j, 0, 0)),
            pl.BlockSpec((tm, N_KV_HEADS * LANES), row),
        ]
    return pl.pallas_call(
        functools.partial(_mixer_in_kernel, sample=sample),
        grid=(batch, nj),
        in_specs=[
            pl.BlockSpec((tm, D_MODEL), row),
            pl.BlockSpec((D_MODEL, IN_PAD), const2),
            pl.BlockSpec((tm, LANES), tab),
            pl.BlockSpec((tm, LANES), tab),
            pl.BlockSpec((tm, LANES), tab),
            pl.BlockSpec((tm, LANES), tab),
            pl.BlockSpec((1, A_WIDTH), const2),
            pl.BlockSpec((1, A_WIDTH), const2),
            pl.BlockSpec((A_GROUPS, CHUNK, CHUNK), const3),
            pl.BlockSpec((CHUNK, A_WIDTH), const2),
        ],
        out_specs=out_specs,
        out_shape=out_shape,
        compiler_params=pltpu.CompilerParams(
            dimension_semantics=("arbitrary", "arbitrary"), vmem_limit_bytes=VMEM_LIMIT),
        name="mixer_in",
    )(x, w_in, cos, sin, cosk, sink, ag, ab, ws, bias)


def _sortable_key(s):
    s = jnp.where(s == 0.0, 0.0, s)
    bits = pltpu.bitcast(s, jnp.int32)
    return jnp.where(bits < 0, bits ^ jnp.int32(0x7FFFFFFF), bits)


KEY_OF_NEG_INF = -2139095041
F32_MIN_NORMAL = 2.0 ** -126


def _sortable_key_by_position(s, pos):
    bits = pltpu.bitcast(s, jnp.int32)
    key = jnp.where(bits < 0, bits ^ jnp.int32(0x7FFFFFFF), bits)
    return jnp.where(jnp.abs(s) < F32_MIN_NORMAL, -1 - pos, key)


def _kth_largest_key(count_ge, rows, k):
    kf = jnp.float32(k)
    zero = jnp.zeros((rows, 1), jnp.int32)
    cur = jnp.where(count_ge(zero) >= kf, zero, jnp.int32(INT_MIN))

    def body(it, cur):
        cand = cur | jnp.left_shift(jnp.int32(1), jnp.int32(30) - it)
        return jnp.where(count_ge(cand) >= kf, cand, cur)

    return lax.fori_loop(0, 31, body, cur)


def _upper_tri_ones():
    r = lax.broadcasted_iota(jnp.int32, (LANES, LANES), 0)
    c = lax.broadcasted_iota(jnp.int32, (LANES, LANES), 1)
    return jnp.where(r <= c, 1.0, 0.0).astype(BF16)


def _tie_break_bias(key_ref, bias_ref, tau, need, n_cols):
    tri = _upper_tri_ones()
    rows = tau.shape[0]
    run = jnp.zeros((rows, 1), F32)
    for c in range(n_cols // LANES):
        cols = slice(c * LANES, (c + 1) * LANES)
        kc = key_ref[:, cols]
        eq = jnp.where(kc == tau, 1.0, 0.0)
        rank = run + jnp.dot(eq.astype(BF16), tri, preferred_element_type=F32)
        keep = (kc > tau) | ((kc == tau) & (rank <= need))
        bias_ref[:, cols] = jnp.where(keep, 0.0, NEG_INF)
        run = run + jnp.sum(eq, axis=-1, keepdims=True)
    return run


INT16_MIN = -2 ** 15
PACK16 = 16
N_COUNT_ACCS = 8


def _halves(ref, r):
    return ref[r * PACK16:(r + 1) * PACK16, :]


def _count_rows_16(mask_of_group, n_rows, lanes):
    one = jnp.ones((PACK16, lanes), jnp.int16)
    zero = jnp.zeros((PACK16, lanes), jnp.int16)
    accs = [zero] * min(N_COUNT_ACCS, n_rows // PACK16)
    for r in range(n_rows // PACK16):
        accs[r % len(accs)] = accs[r % len(accs)] + jnp.where(mask_of_group(r), one, zero)
    parts = accs
    while len(parts) > 1:
        parts = [parts[j] + parts[j + 1] for j in range(0, len(parts), 2)]
    return jnp.sum(parts[0].astype(jnp.int32), axis=0, keepdims=True)


def _bcast16(x, lanes):
    return jnp.broadcast_to(x, (PACK16, lanes)).astype(jnp.int16)


def _kth_largest_16(ref, n_rows, k):
    lanes = ref.shape[1]

    def body(it, cur):
        cand = cur + jnp.left_shift(jnp.int32(1), jnp.int32(15) - it)
        c16 = _bcast16(cand, lanes)
        cnt = _count_rows_16(lambda r: _halves(ref, r) >= c16, n_rows, lanes)
        return jnp.where(cnt >= k, cand, cur)

    return lax.fori_loop(0, 16, body, jnp.full((1, lanes), INT16_MIN, jnp.int32))


def _kth_largest_key_rows(hi_ref, lo_ref, n_rows, k):
    lanes = hi_ref.shape[1]
    tau_hi = _kth_largest_16(hi_ref, n_rows, jnp.int32(k))
    t16 = _bcast16(tau_hi, lanes)
    n_above = _count_rows_16(lambda r: _halves(hi_ref, r) > t16, n_rows, lanes)
    min16 = jnp.full((PACK16, lanes), INT16_MIN, jnp.int16)
    for r in range(n_rows // PACK16):
        lo_ref[r * PACK16:(r + 1) * PACK16, :] = jnp.where(_halves(hi_ref, r) == t16, _halves(lo_ref, r), min16)
    tau_lo = _kth_largest_16(lo_ref, n_rows, jnp.int32(k) - n_above)
    return jnp.left_shift(tau_hi, 16) | (tau_lo - INT16_MIN)


def _lower_tri_ones():
    r = lax.broadcasted_iota(jnp.int32, (LANES, LANES), 0)
    c = lax.broadcasted_iota(jnp.int32, (LANES, LANES), 1)
    return jnp.where(c <= r, 1.0, 0.0).astype(BF16)


def _chunk_rows(c):
    return pl.ds(pl.multiple_of(c * KEY_CHUNK, KEY_CHUNK), KEY_CHUNK)


def _select_topk_bias(i, n_chunks, qit_ref, wit_ref, kib_ref, key_ref, hi_ref, lo_ref, tau_ref, biasq_ref):
    tq = Q_BLOCK
    zeros64 = jnp.zeros((HEAD_DIM, tq), BF16)
    qi_rhs = jnp.concatenate(
        [jnp.concatenate([qit_ref[0, h * IDX_DIM:(h + 1) * IDX_DIM, :], zeros64], axis=0)
         for h in range(N_IDX_HEADS)], axis=1)
    wi = wit_ref[0]
    q_pos = i * tq + lax.broadcasted_iota(jnp.int32, (KEY_CHUNK, tq), 1)
    row_iota = lax.broadcasted_iota(jnp.int32, (KEY_CHUNK, tq), 0)

    def keys_of(c, carry):
        dots = jnp.dot(kib_ref[_chunk_rows(c), :], qi_rhs, preferred_element_type=F32)
        s = jnp.zeros((KEY_CHUNK, tq), F32)
        for h in range(N_IDX_HEADS):
            s = s + wi[h:h + 1, :] * jnp.maximum(dots[:, h * tq:(h + 1) * tq], 0.0)
        key_pos = c * KEY_CHUNK + row_iota
        key = jnp.where(key_pos <= q_pos, _sortable_key_by_position(s, key_pos), jnp.int32(KEY_OF_NEG_INF))
        rows = _chunk_rows(c)
        key_ref[rows, :] = key
        hi_ref[rows, :] = (key >> 16).astype(jnp.int16)
        lo_ref[rows, :] = ((key & 0xFFFF) + INT16_MIN).astype(jnp.int16)
        return carry

    lax.fori_loop(0, n_chunks, keys_of, 0)

    for c in range(1, key_ref.shape[0] // KEY_CHUNK):
        @pl.when(i == c)
        def _(c=c):
            tau_c = _kth_largest_key_rows(hi_ref, lo_ref, (c + 1) * KEY_CHUNK, TOPK_MAX)
            tau_ref[...] = jnp.broadcast_to(tau_c, tau_ref.shape)

    tau = tau_ref[0:1, :]

    def bias_of(c, counts):
        n_ge, n_gt = counts
        key = key_ref[_chunk_rows(c), :]
        ge = key >= tau
        biasq_ref[c] = jnp.where(ge, 0.0, NEG_INF).T
        n_ge = n_ge + jnp.sum(jnp.where(ge, 1.0, 0.0), axis=0, keepdims=True)
        n_gt = n_gt + jnp.sum(jnp.where(key > tau, 1.0, 0.0), axis=0, keepdims=True)
        return n_ge, n_gt

    zero = jnp.zeros((1, tq), F32)
    n_ge, n_gt = lax.fori_loop(0, n_chunks, bias_of, (zero, zero))

    @pl.when(jnp.max(n_ge) > jnp.float32(TOPK_MAX))
    def _():
        need = jnp.float32(TOPK_MAX) - n_gt
        tri = _lower_tri_ones()

        def tie_of(c, run):
            halves = []
            for half in range(KEY_CHUNK // LANES):
                rows = pl.ds(pl.multiple_of(c * KEY_CHUNK + half * LANES, LANES), LANES)
                kc = key_ref[rows, :]
                eq = jnp.where(kc == tau, 1.0, 0.0)
                rank = run + jnp.dot(tri, eq.astype(BF16), preferred_element_type=F32)
                keep = (kc > tau) | ((kc == tau) & (rank <= need))
                halves.append(jnp.where(keep, 0.0, NEG_INF))
                run = run + jnp.sum(eq, axis=0, keepdims=True)
            biasq_ref[c] = jnp.concatenate(halves, axis=0).T
            return run

        lax.fori_loop(0, n_chunks, tie_of, zero)


def _prompt_attn_kernel(q_ref, qit_ref, wit_ref, kib_ref, kdupt_ref, vaug_ref, o_ref,
                        key_ref, hi_ref, lo_ref, tau_ref, biasq_ref, lg_ref, m_ref, acc_ref):
    i = pl.program_id(1)
    n_chunks = i + 1
    tq = Q_BLOCK

    @pl.when(i == 0)
    def _():
        k_pos = lax.broadcasted_iota(jnp.int32, (tq, KEY_CHUNK), 1)
        q_pos = lax.broadcasted_iota(jnp.int32, (tq, KEY_CHUNK), 0)
        biasq_ref[0] = jnp.where(k_pos <= q_pos, 0.0, NEG_INF)

    @pl.when(i > 0)
    def _():
        _select_topk_bias(i, n_chunks, qit_ref, wit_ref, kib_ref, key_ref, hi_ref, lo_ref, tau_ref, biasq_ref)

    lane = lax.broadcasted_iota(jnp.int32, (tq, LANES), 1)
    lo = lane < HEAD_DIM
    q4 = []
    for g in range(N_KV_HEADS):
        parts = []
        for j in range(GROUP_SIZE // 2):
            blk = q_ref[:, g * 2 * LANES + j * LANES:g * 2 * LANES + (j + 1) * LANES].astype(F32)
            parts.append(jnp.where(lo, blk, 0.0).astype(BF16))
            parts.append(jnp.where(lo, 0.0, blk).astype(BF16))
        q4.append(jnp.concatenate(parts, axis=0))
    m_ref[...] = jnp.full(m_ref.shape, NEG_INF, F32)

    def mask_and_max(c, carry):
        bias = biasq_ref[c]
        for g in range(N_KV_HEADS):
            raw = jnp.dot(q4[g], kdupt_ref[0, g, c], preferred_element_type=F32)
            for r in range(GROUP_SIZE):
                rows = slice(r * tq, (r + 1) * tq)
                lg = raw[rows] + bias
                lg_ref[g, c, rows, :] = lg
                m_ref[g, rows, :] = jnp.maximum(m_ref[g, rows, :], jnp.maximum(lg[:, :LANES], lg[:, LANES:]))
        return carry

    lax.fori_loop(0, n_chunks, mask_and_max, 0)
    m_ref[...] = jnp.broadcast_to(jnp.max(m_ref[...], axis=-1, keepdims=True), m_ref.shape)
    acc_ref[...] = jnp.zeros(acc_ref.shape, F32)

    def accumulate(c, carry):
        for g in range(N_KV_HEADS):
            m = m_ref[g]
            lg = lg_ref[g, c]
            p = jnp.concatenate([jnp.exp(lg[:, :LANES] - m), jnp.exp(lg[:, LANES:] - m)], axis=1).astype(BF16)
            acc_ref[g] += jnp.dot(p, vaug_ref[_chunk_rows(c), g * LANES:(g + 1) * LANES],
                                  preferred_element_type=F32)
        return carry

    lax.fori_loop(0, n_chunks, accumulate, 0)
    for g in range(N_KV_HEADS):
        acc = acc_ref[g]
        out = acc * (1.0 / pltpu.roll(acc, HEAD_DIM, 1))
        for j in range(GROUP_SIZE // 2):
            even = out[(2 * j) * tq:(2 * j + 1) * tq]
            odd = pltpu.roll(out[(2 * j + 1) * tq:(2 * j + 2) * tq], HEAD_DIM, 1)
            o_ref[:, g * 2 * LANES + j * LANES:g * 2 * LANES + (j + 1) * LANES] = jnp.where(
                lo, even, odd).astype(BF16)


def _prompt_attention(q, qit, wit, kib, kdupt, vaug, batch, seq):
    nb = seq // Q_BLOCK
    n_q = N_HEADS * HEAD_DIM
    n_qi = N_IDX_HEADS * IDX_DIM
    n_kc = seq // KEY_CHUNK
    qblk = lambda b, i: (b, 0, i)
    perb = lambda b, i: (b, 0)
    return pl.pallas_call(
        _prompt_attn_kernel,
        grid=(batch, nb),
        in_specs=[
            pl.BlockSpec((Q_BLOCK, n_q), lambda b, i: (b * nb + i, 0)),
            pl.BlockSpec((1, n_qi, Q_BLOCK), qblk),
            pl.BlockSpec((1, 8, Q_BLOCK), qblk),
            pl.BlockSpec((seq, LANES), perb),
            pl.BlockSpec((1, N_KV_HEADS, n_kc, LANES, KEY_CHUNK), lambda b, i: (b, 0, 0, 0, 0)),
            pl.BlockSpec((seq, N_KV_HEADS * LANES), perb),
        ],
        out_specs=pl.BlockSpec((Q_BLOCK, n_q), lambda b, i: (b * nb + i, 0)),
        out_shape=jax.ShapeDtypeStruct((batch * seq, n_q), BF16),
        scratch_shapes=[
            pltpu.VMEM((seq, Q_BLOCK), jnp.int32),
            pltpu.VMEM((seq, Q_BLOCK), jnp.int16),
            pltpu.VMEM((seq, Q_BLOCK), jnp.int16),
            pltpu.VMEM((8, Q_BLOCK), jnp.int32),
            pltpu.VMEM((n_kc, Q_BLOCK, KEY_CHUNK), F32),
            pltpu.VMEM((N_KV_HEADS, n_kc, GROUP_SIZE * Q_BLOCK, KEY_CHUNK), F32),
            pltpu.VMEM((N_KV_HEADS, GROUP_SIZE * Q_BLOCK, LANES), F32),
            pltpu.VMEM((N_KV_HEADS, GROUP_SIZE * Q_BLOCK, LANES), F32),
        ],
        compiler_params=pltpu.CompilerParams(
            dimension_semantics=("arbitrary", "arbitrary"), vmem_limit_bytes=VMEM_LIMIT),
        name="prompt_attention",
    )(q, qit, wit, kib, kdupt, vaug)


def _page_copies(pt_ref, b, n_pages, src_hbm, dst_for_page, sem):
    return [pltpu.make_async_copy(src_hbm.at[pt_ref[b * n_pages + p]], dst_for_page(p), sem)
            for p in range(n_pages)]


def _sample_scores_kernel(pt_ref, a_ref, w_ref, kidx_hbm, o_ref, buf, sem, *, n_pages):
    b = pl.program_id(0)
    nb = pl.num_programs(0)
    slot = b % 2

    def copies(bb, sl):
        return _page_copies(pt_ref, bb, n_pages, kidx_hbm,
                            lambda p: buf.at[sl, :, pl.ds(p * PAGE_SIZE, PAGE_SIZE)], sem.at[sl])

    @pl.when(b == 0)
    def _():
        for cp in copies(b, slot):
            cp.start()

    @pl.when(b + 1 < nb)
    def _():
        for cp in copies(b + 1, 1 - slot):
            cp.start()

    for cp in copies(b, slot):
        cp.wait()
    dots = jnp.dot(a_ref[0], buf[slot].astype(BF16), preferred_element_type=F32)
    o_ref[0] = jnp.sum(jnp.maximum(dots, 0.0) * w_ref[0], axis=0, keepdims=True)


def _sample_scores(pt_flat, a, w, kidx_t, n_samples, n_pages):
    past = n_pages * PAGE_SIZE
    return pl.pallas_call(
        functools.partial(_sample_scores_kernel, n_pages=n_pages),
        grid_spec=pltpu.PrefetchScalarGridSpec(
            num_scalar_prefetch=1,
            grid=(n_samples,),
            in_specs=[
                pl.BlockSpec((1, 16, IDX_DIM), lambda b, pt: (b, 0, 0)),
                pl.BlockSpec((1, 16, 1), lambda b, pt: (b, 0, 0)),
                pl.BlockSpec(memory_space=pl.ANY),
            ],
            out_specs=pl.BlockSpec((1, 1, past), lambda b, pt: (b, 0, 0)),
            scratch_shapes=[pltpu.VMEM((2, IDX_DIM, past), F32), pltpu.SemaphoreType.DMA((2,))],
        ),
        out_shape=jax.ShapeDtypeStruct((n_samples, 1, past), F32),
        compiler_params=pltpu.CompilerParams(
            dimension_semantics=("arbitrary",), vmem_limit_bytes=VMEM_LIMIT),
        name="sample_scores",
    )(pt_flat, a, w, kidx_t)


def _sample_select_kernel(s_ref, qi_ref, kiwi_ref, bias_ref, bias_new_ref, key_ref, *, topk):
    rows, past = s_ref.shape
    lane = lax.broadcasted_iota(jnp.int32, (rows, LANES), 1)
    lo = lane < IDX_DIM
    kiwi = kiwi_ref[...]
    kib = kiwi.astype(BF16).astype(F32)
    kidup = jnp.where(lo, kib, pltpu.roll(kib, IDX_DIM, 1))
    s_new = jnp.zeros((rows, 1), F32)
    for j in range(2):
        t = qi_ref[:, j * LANES:(j + 1) * LANES].astype(F32) * kidup
        for half in range(2):
            d = jnp.sum(jnp.where(lo if half == 0 else jnp.logical_not(lo), t, 0.0), axis=-1, keepdims=True)
            h = 2 * j + half
            s_new = s_new + kiwi[:, IDX_DIM + h:IDX_DIM + h + 1] * jnp.maximum(d, 0.0)
    key_new = _sortable_key(s_new)
    key_ref[...] = _sortable_key(s_ref[...])

    def count_ge(c):
        return (jnp.sum(jnp.where(key_ref[...] >= c, 1.0, 0.0), axis=-1, keepdims=True)
                + jnp.where(key_new >= c, 1.0, 0.0))

    tau = _kth_largest_key(count_ge, rows, topk)
    key = key_ref[...]
    bias_ref[...] = jnp.where(key >= tau, 0.0, NEG_INF)
    n_gt = (jnp.sum(jnp.where(key > tau, 1.0, 0.0), axis=-1, keepdims=True)
            + jnp.where(key_new > tau, 1.0, 0.0))
    need = jnp.float32(topk) - n_gt
    bias_new_ref[...] = jnp.broadcast_to(jnp.where(key_new >= tau, 0.0, NEG_INF), (rows, LANES))

    @pl.when(jnp.max(count_ge(tau)) > jnp.float32(topk))
    def _():
        n_eq_past = _tie_break_bias(key_ref, bias_ref, tau, need, past)
        keep_new = (key_new > tau) | ((key_new == tau) & (n_eq_past + 1.0 <= need))
        bias_new_ref[...] = jnp.broadcast_to(jnp.where(keep_new, 0.0, NEG_INF), (rows, LANES))


def _sample_select(scores, qi, kiwi, topk):
    rows, past = scores.shape
    return pl.pallas_call(
        functools.partial(_sample_select_kernel, topk=topk),
        out_shape=[jax.ShapeDtypeStruct((rows, past), F32), jax.ShapeDtypeStruct((rows, LANES), F32)],
        scratch_shapes=[pltpu.VMEM((rows, past), jnp.int32)],
        compiler_params=pltpu.CompilerParams(vmem_limit_bytes=VMEM_LIMIT),
        name="sample_select",
    )(scores, qi, kiwi)


def _sample_attn_kernel(pt_ref, aq_ref, knew_ref, vnew_ref, bias_ref, bias_new_ref, k_hbm, v_hbm, o_ref,
                        kbuf, vbuf, sem, *, n_pages):
    b = pl.program_id(0)
    nb = pl.num_programs(0)
    slot = b % 2

    def copies(bb, sl):
        dst_k = lambda p: kbuf.at[sl, :, :, pl.ds(p * PAGE_SIZE, PAGE_SIZE)]
        dst_v = lambda p: vbuf.at[sl, :, :, pl.ds(p * PAGE_SIZE, PAGE_SIZE)]
        return (_page_copies(pt_ref, bb, n_pages, k_hbm, dst_k, sem.at[0, sl])
                + _page_copies(pt_ref, bb, n_pages, v_hbm, dst_v, sem.at[1, sl]))

    @pl.when(b == 0)
    def _():
        for cp in copies(b, slot):
            cp.start()

    @pl.when(b + 1 < nb)
    def _():
        for cp in copies(b + 1, 1 - slot):
            cp.start()

    for cp in copies(b, slot):
        cp.wait()
    bias = bias_ref[0]
    bias_new = bias_new_ref[0][:, 0:1]
    for g in range(N_KV_HEADS):
        aq = aq_ref[0, g]
        logits = jnp.dot(aq, kbuf[slot, g].astype(BF16), preferred_element_type=F32) + bias
        knew = knew_ref[0, g].astype(BF16).astype(F32)
        l_new = jnp.sum(aq.astype(F32) * knew, axis=-1, keepdims=True) + bias_new
        m = jnp.maximum(jnp.max(logits, axis=-1, keepdims=True), l_new)
        p = jnp.exp(logits - m)
        p_new = jnp.exp(l_new - m)
        l = jnp.sum(p, axis=-1, keepdims=True) + p_new
        pv = lax.dot_general(p.astype(BF16), vbuf[slot, g].astype(BF16), (((1,), (1,)), ((), ())),
                             preferred_element_type=F32)
        pv = pv + p_new.astype(BF16).astype(F32) * vnew_ref[0, g].astype(BF16).astype(F32)
        o_ref[0, g] = pv * (1.0 / l)


def _sample_attention(pt_flat, aq, knew, vnew, bias, bias_new, k_t, v_t, n_samples, n_pages):
    past = n_pages * PAGE_SIZE
    blk4 = lambda b, pt: (b, 0, 0, 0)
    blk3 = lambda b, pt: (b, 0, 0)
    return pl.pallas_call(
        functools.partial(_sample_attn_kernel, n_pages=n_pages),
        grid_spec=pltpu.PrefetchScalarGridSpec(
            num_scalar_prefetch=1,
            grid=(n_samples,),
            in_specs=[
                pl.BlockSpec((1, N_KV_HEADS, 16, HEAD_DIM), blk4),
                pl.BlockSpec((1, N_KV_HEADS, 1, HEAD_DIM), blk4),
                pl.BlockSpec((1, N_KV_HEADS, 1, HEAD_DIM), blk4),
                pl.BlockSpec((1, 1, past), blk3),
                pl.BlockSpec((1, 1, LANES), blk3),
                pl.BlockSpec(memory_space=pl.ANY),
                pl.BlockSpec(memory_space=pl.ANY),
            ],
            out_specs=pl.BlockSpec((1, N_KV_HEADS, 16, HEAD_DIM), blk4),
            scratch_shapes=[
                pltpu.VMEM((2, N_KV_HEADS, HEAD_DIM, past), F32),
                pltpu.VMEM((2, N_KV_HEADS, HEAD_DIM, past), F32),
                pltpu.SemaphoreType.DMA((2, 2)),
            ],
        ),
        out_shape=jax.ShapeDtypeStruct((n_samples, N_KV_HEADS, 16, HEAD_DIM), F32),
        compiler_params=pltpu.CompilerParams(
            dimension_semantics=("arbitrary",), vmem_limit_bytes=VMEM_LIMIT),
        name="sample_attention",
    )(pt_flat, aq, knew, vnew, bias, bias_new, k_t, v_t)


def _rope_tables(pos):
    half = HEAD_DIM // 2
    inv = ROPE_THETA ** (-jnp.arange(half, dtype=F32) / half)
    ang = pos.astype(F32)[:, None] * inv[None, :]
    c = jnp.cos(ang)
    s = jnp.sin(ang)
    cos64 = jnp.concatenate([c, c], axis=1)
    sin64 = jnp.concatenate([-s, s], axis=1)
    n = pos.shape[0]
    cos = jnp.concatenate([cos64, cos64], axis=1)
    sin = jnp.concatenate([sin64, sin64], axis=1)
    cosk = jnp.concatenate([cos64, jnp.full((n, LANES - IDX_DIM), IDX_W_SCALE, F32)], axis=1)
    sink = jnp.concatenate([sin64, jnp.zeros((n, LANES - IDX_DIM), F32)], axis=1)
    return cos, sin, cosk, sink


def kernel(x_prompt, x_sample, cache_k, cache_v, cache_kidx, page_table, ln1_g, ln1_b, ffn1_w_up, ffn1_w_down, ln2_g, ln2_b, w_in, a_ln_g, a_ln_b, a_ws, a_bs, w_out, ln3_g, ln3_b, ffn2_w_up, ffn2_w_down):
    batch, seq, _ = x_prompt.shape
    n_samples = x_sample.shape[0]
    n_pages = page_table.shape[1]
    past = n_pages * PAGE_SIZE
    l = 0

    wgu1, wd1 = ffn1_w_up[l].astype(BF16), ffn1_w_down[l].astype(BF16)
    wgu2, wd2 = ffn2_w_up[l].astype(BF16), ffn2_w_down[l].astype(BF16)
    w_in_p = jnp.pad(w_in[l], ((0, 0), (0, IN_PAD - IN_WIDTH))).astype(BF16)
    w_out_p = w_out[l].reshape(2, A_WIDTH, D_MODEL).astype(BF16)
    causal = jnp.tril(jnp.ones((CHUNK, CHUNK), dtype=bool))
    ws_prompt = jnp.where(causal[None], a_ws[l], 0).astype(BF16)
    bias_prompt = jnp.repeat(a_bs[l].T, A_WIDTH // A_GROUPS, axis=1)
    eye = jnp.eye(CHUNK, dtype=F32)
    ws_sample = (a_ws[l][:, 0, 0][:, None, None] * eye[None]).astype(BF16)
    bias_sample = jnp.broadcast_to(bias_prompt[0:1], (CHUNK, A_WIDTH))

    pos_p = jnp.arange(seq, dtype=jnp.int32)
    pos_s = jnp.full((n_samples,), past, dtype=jnp.int32)
    tabs_p = _rope_tables(pos_p)
    tabs_s = _rope_tables(pos_s)

    xp = x_prompt.reshape(batch * seq, D_MODEL)
    x1p = _ffn_ln(xp, wgu1, wd1, ln1_g, ln1_b, tm=512)
    a_p, kt_p, vt_p, kit_p, q_p, qit_p, wit_p, kib_p, kdupt_p, vaug_p = _mixer_in(
        x1p, w_in_p, *tabs_p, a_ln_g, a_ln_b, ws_prompt, bias_prompt, batch, seq, tm=512, sample=False)
    b_p = _prompt_attention(q_p, qit_p, wit_p, kib_p, kdupt_p, vaug_p, batch, seq)
    yp = _mix_ffn_ln(a_p, b_p, x1p, w_out_p, ln2_g, ln2_b, wgu2, wd2, ln3_g, ln3_b, tm=512)

    xs = x_sample.reshape(n_samples, D_MODEL)
    x1s = _ffn_ln(xs, wgu1, wd1, ln1_g, ln1_b, tm=n_samples)
    a_s, kt_s, vt_s, kit_s, q_s, qi_s, kiwi_s, av_s = _mixer_in(
        x1s, w_in_p, *tabs_s, a_ln_g, a_ln_b, ws_sample, bias_sample, 1, n_samples, tm=n_samples, sample=True)
    pt_flat = page_table.reshape(-1)
    kidx_t = jnp.transpose(cache_kidx[l], (0, 2, 1))
    k_t = jnp.transpose(cache_k[l], (0, 2, 3, 1))
    v_t = jnp.transpose(cache_v[l], (0, 2, 3, 1))
    a_idx = jnp.pad(qi_s.reshape(n_samples, N_IDX_HEADS, IDX_DIM), ((0, 0), (0, 16 - N_IDX_HEADS), (0, 0)))
    w_idx = jnp.pad(kiwi_s[:, IDX_DIM:IDX_DIM + N_IDX_HEADS], ((0, 0), (0, 16 - N_IDX_HEADS)))[..., None]
    scores = _sample_scores(pt_flat, a_idx, w_idx, kidx_t, n_samples, n_pages)
    topk = min(TOPK_MAX, (past + 1) // 4)
    bias_s, bias_new = _sample_select(scores.reshape(n_samples, past), qi_s, kiwi_s, topk)
    aq = jnp.pad(q_s.reshape(n_samples, N_KV_HEADS, GROUP_SIZE, HEAD_DIM),
                 ((0, 0), (0, 0), (0, 16 - GROUP_SIZE), (0, 0)))
    knew = jnp.transpose(kt_s[0], (2, 0, 1))[:, :, None, :]
    vnew = jnp.transpose(vt_s[0], (2, 0, 1))[:, :, None, :]
    o_s = _sample_attention(pt_flat, aq, knew, vnew, bias_s.reshape(n_samples, 1, past),
                            bias_new.reshape(n_samples, 1, LANES), k_t, v_t, n_samples, n_pages)
    b_s = o_s[:, :, :GROUP_SIZE, :].reshape(n_samples, N_HEADS * HEAD_DIM).astype(BF16)
    ys = _mix_ffn_ln(a_s, b_s, x1s, w_out_p, ln2_g, ln2_b, wgu2, wd2, ln3_g, ln3_b, tm=n_samples)

    new_k_p = jnp.transpose(kt_p, (0, 3, 1, 2))[None]
    new_v_p = jnp.transpose(vt_p, (0, 3, 1, 2))[None]
    new_ki_p = jnp.transpose(kit_p, (0, 2, 1))[None]
    new_k_s = jnp.transpose(kt_s[0], (2, 0, 1))[None, :, None]
    new_v_s = jnp.transpose(vt_s[0], (2, 0, 1))[None, :, None]
    new_ki_s = jnp.transpose(kit_s[0], (1, 0))[None, :, None]
    return (yp.reshape(batch, seq, D_MODEL), ys.reshape(n_samples, 1, D_MODEL),
            new_k_p, new_v_p, new_ki_p, new_k_s, new_v_s, new_ki_s,
            av_s.reshape(1, n_samples, 1, A_WIDTH))
```

```python
import functools

import jax
import jax.numpy as jnp
from jax import lax
from jax.experimental import pallas as pl
from jax.experimental.pallas import tpu as pltpu

F32 = jnp.float32
BF16 = jnp.bfloat16

D_MODEL = 1024
D_FF = 2816
A_WIDTH = 512
A_GROUPS = 4
CHUNK = 128
HEAD_DIM = 64
N_HEADS = 8
N_KV_HEADS = 2
GROUP_SIZE = N_HEADS // N_KV_HEADS
N_IDX_HEADS = 4
IDX_DIM = 64
IDX_W_SCALE = (N_IDX_HEADS * IDX_DIM) ** -0.5
TOPK_MAX = 256
PAGE_SIZE = 128
ROPE_THETA = 10000.0
LN_EPS = 1e-5
DEPTH = 1
ALPHA = (2.0 * DEPTH) ** 0.25
ATTN_SCALE = HEAD_DIM ** -0.5

IN_SPLITS = (512, 512, 512, 128, 128, 256, 64, 4)
IN_WIDTH = sum(IN_SPLITS)
LANES = 128
IN_PAD = -(-IN_WIDTH // LANES) * LANES
COL_Q = 1024
COL_K = 1536
COL_V = 1664
COL_QI = 1792
COL_KIWI = 2048

Q_BLOCK = 256
KEY_CHUNK = 256
CHUNKS_PER_TRIP = 4
FF_CHUNK = 256
N_FF_CHUNKS = D_FF // FF_CHUNK
VMEM_LIMIT = 56 * 1024 * 1024
INT_MIN = -2 ** 31
NEG_INF = float("-inf")


def _layernorm(y, g, b):
    mu = jnp.mean(y, axis=-1, keepdims=True)
    d = y - mu
    var = jnp.mean(d * d, axis=-1, keepdims=True)
    return d * lax.rsqrt(var + LN_EPS) * g + b


def _gelu_tanh(x):
    return 0.5 * x * (1.0 + jnp.tanh(0.7978845608028654 * (x + 0.044715 * (x * x * x))))


def _swiglu_ln(x, wup_ref, wd_ref, g_ref, b_ref):
    xb = x.astype(BF16)
    acc = jnp.zeros(x.shape, F32)
    for c in range(N_FF_CHUNKS):
        cols = slice(c * FF_CHUNK, (c + 1) * FF_CHUNK)
        gate = jnp.dot(xb, wup_ref[:, cols], preferred_element_type=F32)
        up = jnp.dot(xb, wup_ref[:, D_FF + c * FF_CHUNK:D_FF + (c + 1) * FF_CHUNK], preferred_element_type=F32)
        h = (gate * (1.0 / (1.0 + jnp.exp(-gate))) * up).astype(BF16)
        acc = acc + jnp.dot(h, wd_ref[cols, :], preferred_element_type=F32)
    return _layernorm(ALPHA * x + 0.5 * acc, g_ref[...], b_ref[...])


def _ffn_ln_kernel(x_ref, wgu_ref, wd_ref, g_ref, b_ref, o_ref):
    o_ref[...] = _swiglu_ln(x_ref[...], wgu_ref, wd_ref, g_ref, b_ref)


def _mix_ffn_ln_kernel(a_ref, b_ref, x_ref, wo_ref, g2_ref, b2_ref, wgu_ref, wd_ref, g3_ref, b3_ref, o_ref):
    mix = (jnp.dot(a_ref[...], wo_ref[0], preferred_element_type=F32)
           + jnp.dot(b_ref[...], wo_ref[1], preferred_element_type=F32))
    x2 = _layernorm(ALPHA * x_ref[...] + mix, g2_ref[...], b2_ref[...])
    o_ref[...] = _swiglu_ln(x2, wgu_ref, wd_ref, g3_ref, b3_ref)


def _resident(shape):
    return pl.BlockSpec(shape, lambda i: (0,) * len(shape), pipeline_mode=pl.Buffered(1))


def _ffn_ln(x, wgu, wd, g, b, tm):
    n = x.shape[0]
    row = lambda i: (i, 0)
    return pl.pallas_call(
        _ffn_ln_kernel,
        grid=(n // tm,),
        in_specs=[
            pl.BlockSpec((tm, D_MODEL), row),
            _resident((D_MODEL, 2 * D_FF)),
            _resident((D_FF, D_MODEL)),
            _resident((1, D_MODEL)),
            _resident((1, D_MODEL)),
        ],
        out_specs=pl.BlockSpec((tm, D_MODEL), row),
        out_shape=jax.ShapeDtypeStruct((n, D_MODEL), F32),
        compiler_params=pltpu.CompilerParams(
            dimension_semantics=("arbitrary",), vmem_limit_bytes=VMEM_LIMIT),
        name="ffn_ln",
    )(x, wgu, wd, g, b)


def _mix_ffn_ln(a, b, x, wo, g2, b2, wgu, wd, g3, b3, tm):
    n = x.shape[0]
    row = lambda i: (i, 0)
    return pl.pallas_call(
        _mix_ffn_ln_kernel,
        grid=(n // tm,),
        in_specs=[
            pl.BlockSpec((tm, A_WIDTH), row),
            pl.BlockSpec((tm, N_HEADS * HEAD_DIM), row),
            pl.BlockSpec((tm, D_MODEL), row),
            _resident((2, A_WIDTH, D_MODEL)),
            _resident((1, D_MODEL)),
            _resident((1, D_MODEL)),
            _resident((D_MODEL, 2 * D_FF)),
            _resident((D_FF, D_MODEL)),
            _resident((1, D_MODEL)),
            _resident((1, D_MODEL)),
        ],
        out_specs=pl.BlockSpec((tm, D_MODEL), row),
        out_shape=jax.ShapeDtypeStruct((n, D_MODEL), F32),
        compiler_params=pltpu.CompilerParams(
            dimension_semantics=("arbitrary",), vmem_limit_bytes=VMEM_LIMIT),
        name="mix_ffn_ln",
    )(a, b, x, wo, g2, b2, wgu, wd, g3, b3)


def _mixer_in_kernel(x_ref, w_ref, cos_ref, sin_ref, cosk_ref, sink_ref, ag_ref, ab_ref, ws_ref,
                     bias_ref, aout_ref, kt_ref, vt_ref, kit_ref, *mode_refs, sample):
    tm = x_ref.shape[0]
    xb = x_ref[...].astype(BF16)

    def proj(lo, hi):
        return jnp.dot(xb, w_ref[:, lo:hi], preferred_element_type=F32)

    u = _gelu_tanh(proj(0, A_WIDTH))
    vn = _layernorm(_gelu_tanh(proj(A_WIDTH, 2 * A_WIDTH)), ag_ref[...], ab_ref[...])
    vnb = vn.astype(BF16)
    for c in range(tm // CHUNK):
        rows = slice(c * CHUNK, (c + 1) * CHUNK)
        for g in range(A_GROUPS):
            cols = slice(g * LANES, (g + 1) * LANES)
            mixed = jnp.dot(ws_ref[g], vnb[rows, cols], preferred_element_type=F32) + bias_ref[:, cols]
            aout_ref[rows, cols] = (u[rows, cols] * mixed).astype(BF16)

    lane = lax.broadcasted_iota(jnp.int32, (tm, LANES), 1)
    first_half = (lane & (HEAD_DIM - 1)) < (HEAD_DIM // 2)

    def rope(x, cos, sin):
        partner = jnp.where(first_half, pltpu.roll(x, LANES - HEAD_DIM // 2, 1),
                            pltpu.roll(x, HEAD_DIM // 2, 1))
        return x * cos + partner * sin

    cos = cos_ref[...]
    sin = sin_ref[...]
    qall = proj(COL_Q, COL_K)
    rest = proj(COL_K, IN_PAD)
    k = rope(rest[:, 0:128], cos, sin)
    v = rest[:, 128:256]
    kiwi = rope(rest[:, 512:640], cosk_ref[...], sink_ref[...])
    k_t = k.T
    v_t = v.T
    kiwi_t = kiwi.T
    kt_ref[0] = k_t.reshape(N_KV_HEADS, HEAD_DIM, tm)
    vt_ref[0] = v_t.reshape(N_KV_HEADS, HEAD_DIM, tm)
    kit_ref[0] = kiwi_t[0:IDX_DIM]
    if sample:
        q_ref, qi_ref, kiwi_ref, av_ref = mode_refs
        av_ref[...] = vn
        kiwi_ref[...] = kiwi
        for j in range(4):
            cols = slice(j * LANES, (j + 1) * LANES)
            q_ref[:, cols] = (rope(qall[:, cols], cos, sin) * ATTN_SCALE).astype(BF16)
        for j in range(2):
            cols = slice(j * LANES, (j + 1) * LANES)
            qi_ref[:, cols] = rope(rest[:, 256 + j * LANES:256 + (j + 1) * LANES], cos, sin).astype(BF16)
    else:
        q_ref, qit_ref, wit_ref, kib_ref, kdupt_ref, vaug_ref = mode_refs
        for j in range(4):
            cols = slice(j * LANES, (j + 1) * LANES)
            q_ref[:, cols] = (rope(qall[:, cols], cos, sin) * ATTN_SCALE).astype(BF16)
        for j in range(2):
            cols = slice(j * LANES, (j + 1) * LANES)
            qit_ref[0, cols, :] = rope(rest[:, 256 + j * LANES:256 + (j + 1) * LANES], cos, sin).T.astype(BF16)
        wit_ref[0] = kiwi_t[IDX_DIM:IDX_DIM + 8]
        kib_ref[...] = kiwi.astype(BF16)
        lo = lane < HEAD_DIM
        k_tb = k_t.astype(BF16)
        for g in range(N_KV_HEADS):
            kg = k_tb[g * HEAD_DIM:(g + 1) * HEAD_DIM]
            kg2 = jnp.concatenate([kg, kg], axis=0)
            for cc in range(tm // KEY_CHUNK):
                kdupt_ref[0, g, cc] = kg2[:, cc * KEY_CHUNK:(cc + 1) * KEY_CHUNK]
        vaug_ref[:, 0:LANES] = jnp.where(lo, v, 1.0).astype(BF16)
        vaug_ref[:, LANES:2 * LANES] = jnp.where(lo, pltpu.roll(v, HEAD_DIM, 1), 1.0).astype(BF16)


def _mixer_in(x, w_in, cos, sin, cosk, sink, ag, ab, ws, bias, batch, seq, tm, sample):
    n = batch * seq
    nj = seq // tm
    row = lambda b, j: (b * nj + j, 0)
    tab = lambda b, j: (j, 0)
    const2 = lambda b, j: (0, 0)
    const3 = lambda b, j: (0, 0, 0)
    fmajor = lambda b, j: (b, 0, j)
    n_q = N_HEADS * HEAD_DIM
    n_qi = N_IDX_HEADS * IDX_DIM
    out_shape = [
        jax.ShapeDtypeStruct((n, A_WIDTH), BF16),
        jax.ShapeDtypeStruct((batch, N_KV_HEADS, HEAD_DIM, seq), F32),
        jax.ShapeDtypeStruct((batch, N_KV_HEADS, HEAD_DIM, seq), F32),
        jax.ShapeDtypeStruct((batch, IDX_DIM, seq), F32),
    ]
    out_specs = [
        pl.BlockSpec((tm, A_WIDTH), row),
        pl.BlockSpec((1, N_KV_HEADS, HEAD_DIM, tm), lambda b, j: (b, 0, 0, j)),
        pl.BlockSpec((1, N_KV_HEADS, HEAD_DIM, tm), lambda b, j: (b, 0, 0, j)),
        pl.BlockSpec((1, IDX_DIM, tm), fmajor),
    ]
    if sample:
        out_shape += [
            jax.ShapeDtypeStruct((n, n_q), BF16),
            jax.ShapeDtypeStruct((n, n_qi), BF16),
            jax.ShapeDtypeStruct((n, LANES), F32),
            jax.ShapeDtypeStruct((n, A_WIDTH), F32),
        ]
        out_specs += [
            pl.BlockSpec((tm, n_q), row),
            pl.BlockSpec((tm, n_qi), row),
            pl.BlockSpec((tm, LANES), row),
            pl.BlockSpec((tm, A_WIDTH), row),
        ]
    else:
        out_shape += [
            jax.ShapeDtypeStruct((n, n_q), BF16),
            jax.ShapeDtypeStruct((batch, n_qi, seq), BF16),
            jax.ShapeDtypeStruct((batch, 8, seq), F32),
            jax.ShapeDtypeStruct((n, LANES), BF16),
            jax.ShapeDtypeStruct((batch, N_KV_HEADS, seq // KEY_CHUNK, LANES, KEY_CHUNK), BF16),
            jax.ShapeDtypeStruct((n, N_KV_HEADS * LANES), BF16),
        ]
        out_specs += [
            pl.BlockSpec((tm, n_q), row),
            pl.BlockSpec((1, n_qi, tm), fmajor),
            pl.BlockSpec((1, 8, tm), fmajor),
            pl.BlockSpec((tm, LANES), row),
            pl.BlockSpec((1, N_KV_HEADS, tm // KEY_CHUNK, LANES, KEY_CHUNK), lambda b, j: (b, 0, j, 0, 0)),
            pl.BlockSpec((tm, N_KV_HEADS * LANES), row),
        ]
    return pl.pallas_call(
        functools.partial(_mixer_in_kernel, sample=sample),
        grid=(batch, nj),
        in_specs=[
            pl.BlockSpec((tm, D_MODEL), row),
            pl.BlockSpec((D_MODEL, IN_PAD), const2),
            pl.BlockSpec((tm, LANES), tab),
            pl.BlockSpec((tm, LANES), tab),
            pl.BlockSpec((tm, LANES), tab),
            pl.BlockSpec((tm, LANES), tab),
            pl.BlockSpec((1, A_WIDTH), const2),
            pl.BlockSpec((1, A_WIDTH), const2),
            pl.BlockSpec((A_GROUPS, CHUNK, CHUNK), const3),
            pl.BlockSpec((CHUNK, A_WIDTH), const2),
        ],
        out_specs=out_specs,
        out_shape=out_shape,
        compiler_params=pltpu.CompilerParams(
            dimension_semantics=("arbitrary", "arbitrary"), vmem_limit_bytes=VMEM_LIMIT),
        name="mixer_in",
    )(x, w_in, cos, sin, cosk, sink, ag, ab, ws, bias)


def _sortable_key(s):
    s = jnp.where(s == 0.0, 0.0, s)
    bits = pltpu.bitcast(s, jnp.int32)
    return jnp.where(bits < 0, bits ^ jnp.int32(0x7FFFFFFF), bits)


KEY_OF_NEG_INF = -2139095041
F32_MIN_NORMAL = 2.0 ** -126


def _sortable_key_by_position(s, pos):
    bits = pltpu.bitcast(s, jnp.int32)
    key = jnp.where(bits < 0, bits ^ jnp.int32(0x7FFFFFFF), bits)
    return jnp.where(jnp.abs(s) < F32_MIN_NORMAL, -1 - pos, key)


def _kth_largest_key(count_ge, rows, k):
    kf = jnp.float32(k)
    zero = jnp.zeros((rows, 1), jnp.int32)
    cur = jnp.where(count_ge(zero) >= kf, zero, jnp.int32(INT_MIN))

    def body(it, cur):
        cand = cur | jnp.left_shift(jnp.int32(1), jnp.int32(30) - it)
        return jnp.where(count_ge(cand) >= kf, cand, cur)

    return lax.fori_loop(0, 31, body, cur)


def _upper_tri_ones():
    r = lax.broadcasted_iota(jnp.int32, (LANES, LANES), 0)
    c = lax.broadcasted_iota(jnp.int32, (LANES, LANES), 1)
    return jnp.where(r <= c, 1.0, 0.0).astype(BF16)


def _tie_break_bias(key_ref, bias_ref, tau, need, n_cols):
    tri = _upper_tri_ones()
    rows = tau.shape[0]
    run = jnp.zeros((rows, 1), F32)
    for c in range(n_cols // LANES):
        cols = slice(c * LANES, (c + 1) * LANES)
        kc = key_ref[:, cols]
        eq = jnp.where(kc == tau, 1.0, 0.0)
        rank = run + jnp.dot(eq.astype(BF16), tri, preferred_element_type=F32)
        keep = (kc > tau) | ((kc == tau) & (rank <= need))
        bias_ref[:, cols] = jnp.where(keep, 0.0, NEG_INF)
        run = run + jnp.sum(eq, axis=-1, keepdims=True)
    return run


INT16_MIN = -2 ** 15
PACK16 = 16
N_COUNT_ACCS = 8


def _halves(ref, r):
    return ref[r * PACK16:(r + 1) * PACK16, :]


def _count_rows_16(mask_of_group, n_rows, lanes):
    one = jnp.ones((PACK16, lanes), jnp.int16)
    zero = jnp.zeros((PACK16, lanes), jnp.int16)
    accs = [zero] * min(N_COUNT_ACCS, n_rows // PACK16)
    for r in range(n_rows // PACK16):
        accs[r % len(accs)] = accs[r % len(accs)] + jnp.where(mask_of_group(r), one, zero)
    parts = accs
    while len(parts) > 1:
        parts = [parts[j] + parts[j + 1] for j in range(0, len(parts), 2)]
    return jnp.sum(parts[0].astype(jnp.int32), axis=0, keepdims=True)


def _bcast16(x, lanes):
    return jnp.broadcast_to(x, (PACK16, lanes)).astype(jnp.int16)


def _kth_largest_16(ref, n_rows, k):
    lanes = ref.shape[1]

    def body(it, cur):
        cand = cur + jnp.left_shift(jnp.int32(1), jnp.int32(15) - it)
        c16 = _bcast16(cand, lanes)
        cnt = _count_rows_16(lambda r: _halves(ref, r) >= c16, n_rows, lanes)
        return jnp.where(cnt >= k, cand, cur)

    return lax.fori_loop(0, 16, body, jnp.full((1, lanes), INT16_MIN, jnp.int32))


def _kth_largest_key_rows(hi_ref, lo_ref, n_rows, k):
    lanes = hi_ref.shape[1]
    tau_hi = _kth_largest_16(hi_ref, n_rows, jnp.int32(k))
    t16 = _bcast16(tau_hi, lanes)
    n_above = _count_rows_16(lambda r: _halves(hi_ref, r) > t16, n_rows, lanes)
    min16 = jnp.full((PACK16, lanes), INT16_MIN, jnp.int16)
    for r in range(n_rows // PACK16):
        lo_ref[r * PACK16:(r + 1) * PACK16, :] = jnp.where(_halves(hi_ref, r) == t16, _halves(lo_ref, r), min16)
    tau_lo = _kth_largest_16(lo_ref, n_rows, jnp.int32(k) - n_above)
    return jnp.left_shift(tau_hi, 16) | (tau_lo - INT16_MIN)


def _lower_tri_ones():
    r = lax.broadcasted_iota(jnp.int32, (LANES, LANES), 0)
    c = lax.broadcasted_iota(jnp.int32, (LANES, LANES), 1)
    return jnp.where(c <= r, 1.0, 0.0).astype(BF16)


def _for_each_chunk(n_chunks, body):
    def trip(t, carry):
        for j in range(CHUNKS_PER_TRIP):
            body(CHUNKS_PER_TRIP * t + j)
        return carry

    lax.fori_loop(0, n_chunks // CHUNKS_PER_TRIP, trip, 0)
    done = (n_chunks // CHUNKS_PER_TRIP) * CHUNKS_PER_TRIP
    size = CHUNKS_PER_TRIP // 2
    while size >= 1:
        take = ((n_chunks - done) & size) != 0

        @pl.when(take)
        def _(done=done, size=size):
            for j in range(size):
                body(done + j)

        done = done + jnp.where(take, size, 0)
        size //= 2


def _chunk_rows(c):
    return pl.ds(pl.multiple_of(c * KEY_CHUNK, KEY_CHUNK), KEY_CHUNK)


def _select_topk_bias(i, n_chunks, qit_ref, wit_ref, kib_ref, key_ref, hi_ref, lo_ref, tau_ref, biasq_ref):
    tq = Q_BLOCK
    zeros64 = jnp.zeros((HEAD_DIM, tq), BF16)
    qi_rhs = jnp.concatenate(
        [jnp.concatenate([qit_ref[0, h * IDX_DIM:(h + 1) * IDX_DIM, :], zeros64], axis=0)
         for h in range(N_IDX_HEADS)], axis=1)
    wi = wit_ref[0]
    q_pos = i * tq + lax.broadcasted_iota(jnp.int32, (KEY_CHUNK, tq), 1)
    row_iota = lax.broadcasted_iota(jnp.int32, (KEY_CHUNK, tq), 0)

    def keys_of(c):
        dots = jnp.dot(kib_ref[_chunk_rows(c), :], qi_rhs, preferred_element_type=F32)
        s = jnp.zeros((KEY_CHUNK, tq), F32)
        for h in range(N_IDX_HEADS):
            s = s + wi[h:h + 1, :] * jnp.maximum(dots[:, h * tq:(h + 1) * tq], 0.0)
        key_pos = c * KEY_CHUNK + row_iota
        key = jnp.where(key_pos <= q_pos, _sortable_key_by_position(s, key_pos), jnp.int32(KEY_OF_NEG_INF))
        rows = _chunk_rows(c)
        key_ref[rows, :] = key
        hi_ref[rows, :] = (key >> 16).astype(jnp.int16)
        lo_ref[rows, :] = ((key & 0xFFFF) + INT16_MIN).astype(jnp.int16)

    _for_each_chunk(n_chunks, keys_of)

    for c in range(1, key_ref.shape[0] // KEY_CHUNK):
        @pl.when(i == c)
        def _(c=c):
            tau_c = _kth_largest_key_rows(hi_ref, lo_ref, (c + 1) * KEY_CHUNK, TOPK_MAX)
            tau_ref[...] = jnp.broadcast_to(tau_c, tau_ref.shape)

    tau = tau_ref[0:1, :]

    def bias_of(c, counts):
        n_ge, n_gt = counts
        key = key_ref[_chunk_rows(c), :]
        ge = key >= tau
        biasq_ref[c] = jnp.where(ge, 0.0, NEG_INF).T
        n_ge = n_ge + jnp.sum(jnp.where(ge, 1.0, 0.0), axis=0, keepdims=True)
        n_gt = n_gt + jnp.sum(jnp.where(key > tau, 1.0, 0.0), axis=0, keepdims=True)
        return n_ge, n_gt

    zero = jnp.zeros((1, tq), F32)
    n_ge, n_gt = lax.fori_loop(0, n_chunks, bias_of, (zero, zero))

    @pl.when(jnp.max(n_ge) > jnp.float32(TOPK_MAX))
    def _():
        need = jnp.float32(TOPK_MAX) - n_gt
        tri = _lower_tri_ones()

        def tie_of(c, run):
            halves = []
            for half in range(KEY_CHUNK // LANES):
                rows = pl.ds(pl.multiple_of(c * KEY_CHUNK + half * LANES, LANES), LANES)
                kc = key_ref[rows, :]
                eq = jnp.where(kc == tau, 1.0, 0.0)
                rank = run + jnp.dot(tri, eq.astype(BF16), preferred_element_type=F32)
                keep = (kc > tau) | ((kc == tau) & (rank <= need))
                halves.append(jnp.where(keep, 0.0, NEG_INF))
                run = run + jnp.sum(eq, axis=0, keepdims=True)
            biasq_ref[c] = jnp.concatenate(halves, axis=0).T
            return run

        lax.fori_loop(0, n_chunks, tie_of, zero)


def _prompt_attn_kernel(q_ref, qit_ref, wit_ref, kib_ref, kdupt_ref, vaug_ref, o_ref,
                        key_ref, hi_ref, lo_ref, tau_ref, biasq_ref, lg_ref, m_ref, acc_ref):
    i = pl.program_id(1)
    n_chunks = i + 1
    tq = Q_BLOCK

    @pl.when(i == 0)
    def _():
        k_pos = lax.broadcasted_iota(jnp.int32, (tq, KEY_CHUNK), 1)
        q_pos = lax.broadcasted_iota(jnp.int32, (tq, KEY_CHUNK), 0)
        biasq_ref[0] = jnp.where(k_pos <= q_pos, 0.0, NEG_INF)

    @pl.when(i > 0)
    def _():
        _select_topk_bias(i, n_chunks, qit_ref, wit_ref, kib_ref, key_ref, hi_ref, lo_ref, tau_ref, biasq_ref)

    lane = lax.broadcasted_iota(jnp.int32, (tq, LANES), 1)
    lo = lane < HEAD_DIM
    q4 = []
    for g in range(N_KV_HEADS):
        parts = []
        for j in range(GROUP_SIZE // 2):
            blk = q_ref[:, g * 2 * LANES + j * LANES:g * 2 * LANES + (j + 1) * LANES].astype(F32)
            parts.append(jnp.where(lo, blk, 0.0).astype(BF16))
            parts.append(jnp.where(lo, 0.0, blk).astype(BF16))
        q4.append(jnp.concatenate(parts, axis=0))
    m_ref[...] = jnp.full(m_ref.shape, NEG_INF, F32)

    def mask_and_max(c):
        bias = biasq_ref[c]
        for g in range(N_KV_HEADS):
            raw = jnp.dot(q4[g], kdupt_ref[0, g, c], preferred_element_type=F32)
            for r in range(GROUP_SIZE):
                rows = slice(r * tq, (r + 1) * tq)
                lg = raw[rows] + bias
                lg_ref[g, c, rows, :] = lg
                m_ref[g, rows, :] = jnp.maximum(m_ref[g, rows, :], jnp.maximum(lg[:, :LANES], lg[:, LANES:]))

    _for_each_chunk(n_chunks, mask_and_max)
    m_ref[...] = jnp.broadcast_to(jnp.max(m_ref[...], axis=-1, keepdims=True), m_ref.shape)
    acc_ref[...] = jnp.zeros(acc_ref.shape, F32)

    def accumulate(c):
        for g in range(N_KV_HEADS):
            m = m_ref[g]
            lg = lg_ref[g, c]
            p = jnp.concatenate([jnp.exp(lg[:, :LANES] - m), jnp.exp(lg[:, LANES:] - m)], axis=1).astype(BF16)
            acc_ref[g] += jnp.dot(p, vaug_ref[_chunk_rows(c), g * LANES:(g + 1) * LANES],
                                  preferred_element_type=F32)

    _for_each_chunk(n_chunks, accumulate)
    for g in range(N_KV_HEADS):
        acc = acc_ref[g]
        out = acc * (1.0 / pltpu.roll(acc, HEAD_DIM, 1))
        for j in range(GROUP_SIZE // 2):
            even = out[(2 * j) * tq:(2 * j + 1) * tq]
            odd = pltpu.roll(out[(2 * j + 1) * tq:(2 * j + 2) * tq], HEAD_DIM, 1)
            o_ref[:, g * 2 * LANES + j * LANES:g * 2 * LANES + (j + 1) * LANES] = jnp.where(
                lo, even, odd).astype(BF16)


def _prompt_attention(q, qit, wit, kib, kdupt, vaug, batch, seq):
    nb = seq // Q_BLOCK
    n_q = N_HEADS * HEAD_DIM
    n_qi = N_IDX_HEADS * IDX_DIM
    n_kc = seq // KEY_CHUNK
    qblk = lambda b, i: (b, 0, i)
    perb = lambda b, i: (b, 0)
    return pl.pallas_call(
        _prompt_attn_kernel,
        grid=(batch, nb),
        in_specs=[
            pl.BlockSpec((Q_BLOCK, n_q), lambda b, i: (b * nb + i, 0)),
            pl.BlockSpec((1, n_qi, Q_BLOCK), qblk),
            pl.BlockSpec((1, 8, Q_BLOCK), qblk),
            pl.BlockSpec((seq, LANES), perb),
            pl.BlockSpec((1, N_KV_HEADS, n_kc, LANES, KEY_CHUNK), lambda b, i: (b, 0, 0, 0, 0)),
            pl.BlockSpec((seq, N_KV_HEADS * LANES), perb),
        ],
        out_specs=pl.BlockSpec((Q_BLOCK, n_q), lambda b, i: (b * nb + i, 0)),
        out_shape=jax.ShapeDtypeStruct((batch * seq, n_q), BF16),
        scratch_shapes=[
            pltpu.VMEM((seq, Q_BLOCK), jnp.int32),
            pltpu.VMEM((seq, Q_BLOCK), jnp.int16),
            pltpu.VMEM((seq, Q_BLOCK), jnp.int16),
            pltpu.VMEM((8, Q_BLOCK), jnp.int32),
            pltpu.VMEM((n_kc, Q_BLOCK, KEY_CHUNK), F32),
            pltpu.VMEM((N_KV_HEADS, n_kc, GROUP_SIZE * Q_BLOCK, KEY_CHUNK), F32),
            pltpu.VMEM((N_KV_HEADS, GROUP_SIZE * Q_BLOCK, LANES), F32),
            pltpu.VMEM((N_KV_HEADS, GROUP_SIZE * Q_BLOCK, LANES), F32),
        ],
        compiler_params=pltpu.CompilerParams(
            dimension_semantics=("arbitrary", "arbitrary"), vmem_limit_bytes=VMEM_LIMIT),
        name="prompt_attention",
    )(q, qit, wit, kib, kdupt, vaug)


def _page_copies(pt_ref, b, n_pages, src_hbm, dst_for_page, sem):
    return [pltpu.make_async_copy(src_hbm.at[pt_ref[b * n_pages + p]], dst_for_page(p), sem)
            for p in range(n_pages)]


def _sample_scores_kernel(pt_ref, a_ref, w_ref, kidx_hbm, o_ref, buf, sem, *, n_pages):
    b = pl.program_id(0)
    nb = pl.num_programs(0)
    slot = b % 2

    def copies(bb, sl):
        return _page_copies(pt_ref, bb, n_pages, kidx_hbm,
                            lambda p: buf.at[sl, :, pl.ds(p * PAGE_SIZE, PAGE_SIZE)], sem.at[sl])

    @pl.when(b == 0)
    def _():
        for cp in copies(b, slot):
            cp.start()

    @pl.when(b + 1 < nb)
    def _():
        for cp in copies(b + 1, 1 - slot):
            cp.start()

    for cp in copies(b, slot):
        cp.wait()
    dots = jnp.dot(a_ref[0], buf[slot].astype(BF16), preferred_element_type=F32)
    o_ref[0] = jnp.sum(jnp.maximum(dots, 0.0) * w_ref[0], axis=0, keepdims=True)


def _sample_scores(pt_flat, a, w, kidx_t, n_samples, n_pages):
    past = n_pages * PAGE_SIZE
    return pl.pallas_call(
        functools.partial(_sample_scores_kernel, n_pages=n_pages),
        grid_spec=pltpu.PrefetchScalarGridSpec(
            num_scalar_prefetch=1,
            grid=(n_samples,),
            in_specs=[
                pl.BlockSpec((1, 16, IDX_DIM), lambda b, pt: (b, 0, 0)),
                pl.BlockSpec((1, 16, 1), lambda b, pt: (b, 0, 0)),
                pl.BlockSpec(memory_space=pl.ANY),
            ],
            out_specs=pl.BlockSpec((1, 1, past), lambda b, pt: (b, 0, 0)),
            scratch_shapes=[pltpu.VMEM((2, IDX_DIM, past), F32), pltpu.SemaphoreType.DMA((2,))],
        ),
        out_shape=jax.ShapeDtypeStruct((n_samples, 1, past), F32),
        compiler_params=pltpu.CompilerParams(
            dimension_semantics=("arbitrary",), vmem_limit_bytes=VMEM_LIMIT),
        name="sample_scores",
    )(pt_flat, a, w, kidx_t)


def _sample_select_kernel(s_ref, qi_ref, kiwi_ref, bias_ref, bias_new_ref, key_ref, *, topk):
    rows, past = s_ref.shape
    lane = lax.broadcasted_iota(jnp.int32, (rows, LANES), 1)
    lo = lane < IDX_DIM
    kiwi = kiwi_ref[...]
    kib = kiwi.astype(BF16).astype(F32)
    kidup = jnp.where(lo, kib, pltpu.roll(kib, IDX_DIM, 1))
    s_new = jnp.zeros((rows, 1), F32)
    for j in range(2):
        t = qi_ref[:, j * LANES:(j + 1) * LANES].astype(F32) * kidup
        for half in range(2):
            d = jnp.sum(jnp.where(lo if half == 0 else jnp.logical_not(lo), t, 0.0), axis=-1, keepdims=True)
            h = 2 * j + half
            s_new = s_new + kiwi[:, IDX_DIM + h:IDX_DIM + h + 1] * jnp.maximum(d, 0.0)
    key_new = _sortable_key(s_new)
    key_ref[...] = _sortable_key(s_ref[...])

    def count_ge(c):
        return (jnp.sum(jnp.where(key_ref[...] >= c, 1.0, 0.0), axis=-1, keepdims=True)
                + jnp.where(key_new >= c, 1.0, 0.0))

    tau = _kth_largest_key(count_ge, rows, topk)
    key = key_ref[...]
    bias_ref[...] = jnp.where(key >= tau, 0.0, NEG_INF)
    n_gt = (jnp.sum(jnp.where(key > tau, 1.0, 0.0), axis=-1, keepdims=True)
            + jnp.where(key_new > tau, 1.0, 0.0))
    need = jnp.float32(topk) - n_gt
    bias_new_ref[...] = jnp.broadcast_to(jnp.where(key_new >= tau, 0.0, NEG_INF), (rows, LANES))

    @pl.when(jnp.max(count_ge(tau)) > jnp.float32(topk))
    def _():
        n_eq_past = _tie_break_bias(key_ref, bias_ref, tau, need, past)
        keep_new = (key_new > tau) | ((key_new == tau) & (n_eq_past + 1.0 <= need))
        bias_new_ref[...] = jnp.broadcast_to(jnp.where(keep_new, 0.0, NEG_INF), (rows, LANES))


def _sample_select(scores, qi, kiwi, topk):
    rows, past = scores.shape
    return pl.pallas_call(
        functools.partial(_sample_select_kernel, topk=topk),
        out_shape=[jax.ShapeDtypeStruct((rows, past), F32), jax.ShapeDtypeStruct((rows, LANES), F32)],
        scratch_shapes=[pltpu.VMEM((rows, past), jnp.int32)],
        compiler_params=pltpu.CompilerParams(vmem_limit_bytes=VMEM_LIMIT),
        name="sample_select",
    )(scores, qi, kiwi)


def _sample_attn_kernel(pt_ref, aq_ref, knew_ref, vnew_ref, bias_ref, bias_new_ref, k_hbm, v_hbm, o_ref,
                        kbuf, vbuf, sem, *, n_pages):
    b = pl.program_id(0)
    nb = pl.num_programs(0)
    slot = b % 2

    def copies(bb, sl):
        dst_k = lambda p: kbuf.at[sl, :, :, pl.ds(p * PAGE_SIZE, PAGE_SIZE)]
        dst_v = lambda p: vbuf.at[sl, :, :, pl.ds(p * PAGE_SIZE, PAGE_SIZE)]
        return (_page_copies(pt_ref, bb, n_pages, k_hbm, dst_k, sem.at[0, sl])
                + _page_copies(pt_ref, bb, n_pages, v_hbm, dst_v, sem.at[1, sl]))

    @pl.when(b == 0)
    def _():
        for cp in copies(b, slot):
            cp.start()

    @pl.when(b + 1 < nb)
    def _():
        for cp in copies(b + 1, 1 - slot):
            cp.start()

    for cp in copies(b, slot):
        cp.wait()
    bias = bias_ref[0]
    bias_new = bias_new_ref[0][:, 0:1]
    for g in range(N_KV_HEADS):
        aq = aq_ref[0, g]
        logits = jnp.dot(aq, kbuf[slot, g].astype(BF16), preferred_element_type=F32) + bias
        knew = knew_ref[0, g].astype(BF16).astype(F32)
        l_new = jnp.sum(aq.astype(F32) * knew, axis=-1, keepdims=True) + bias_new
        m = jnp.maximum(jnp.max(logits, axis=-1, keepdims=True), l_new)
        p = jnp.exp(logits - m)
        p_new = jnp.exp(l_new - m)
        l = jnp.sum(p, axis=-1, keepdims=True) + p_new
        pv = lax.dot_general(p.astype(BF16), vbuf[slot, g].astype(BF16), (((1,), (1,)), ((), ())),
                             preferred_element_type=F32)
        pv = pv + p_new.astype(BF16).astype(F32) * vnew_ref[0, g].astype(BF16).astype(F32)
        o_ref[0, g] = pv * (1.0 / l)


def _sample_attention(pt_flat, aq, knew, vnew, bias, bias_new, k_t, v_t, n_samples, n_pages):
    past = n_pages * PAGE_SIZE
    blk4 = lambda b, pt: (b, 0, 0, 0)
    blk3 = lambda b, pt: (b, 0, 0)
    return pl.pallas_call(
        functools.partial(_sample_attn_kernel, n_pages=n_pages),
        grid_spec=pltpu.PrefetchScalarGridSpec(
            num_scalar_prefetch=1,
            grid=(n_samples,),
            in_specs=[
                pl.BlockSpec((1, N_KV_HEADS, 16, HEAD_DIM), blk4),
                pl.BlockSpec((1, N_KV_HEADS, 1, HEAD_DIM), blk4),
                pl.BlockSpec((1, N_KV_HEADS, 1, HEAD_DIM), blk4),
                pl.BlockSpec((1, 1, past), blk3),
                pl.BlockSpec((1, 1, LANES), blk3),
                pl.BlockSpec(memory_space=pl.ANY),
                pl.BlockSpec(memory_space=pl.ANY),
            ],
            out_specs=pl.BlockSpec((1, N_KV_HEADS, 16, HEAD_DIM), blk4),
            scratch_shapes=[
                pltpu.VMEM((2, N_KV_HEADS, HEAD_DIM, past), F32),
                pltpu.VMEM((2, N_KV_HEADS, HEAD_DIM, past), F32),
                pltpu.SemaphoreType.DMA((2, 2)),
            ],
        ),
        out_shape=jax.ShapeDtypeStruct((n_samples, N_KV_HEADS, 16, HEAD_DIM), F32),
        compiler_params=pltpu.CompilerParams(
            dimension_semantics=("arbitrary",), vmem_limit_bytes=VMEM_LIMIT),
        name="sample_attention",
    )(pt_flat, aq, knew, vnew, bias, bias_new, k_t, v_t)


def _rope_tables(pos):
    half = HEAD_DIM // 2
    inv = ROPE_THETA ** (-jnp.arange(half, dtype=F32) / half)
    ang = pos.astype(F32)[:, None] * inv[None, :]
    c = jnp.cos(ang)
    s = jnp.sin(ang)
    cos64 = jnp.concatenate([c, c], axis=1)
    sin64 = jnp.concatenate([-s, s], axis=1)
    n = pos.shape[0]
    cos = jnp.concatenate([cos64, cos64], axis=1)
    sin = jnp.concatenate([sin64, sin64], axis=1)
    cosk = jnp.concatenate([cos64, jnp.full((n, LANES - IDX_DIM), IDX_W_SCALE, F32)], axis=1)
    sink = jnp.concatenate([sin64, jnp.zeros((n, LANES - IDX_DIM), F32)], axis=1)
    return cos, sin, cosk, sink


def kernel(x_prompt, x_sample, cache_k, cache_v, cache_kidx, page_table, ln1_g, ln1_b, ffn1_w_up, ffn1_w_down, ln2_g, ln2_b, w_in, a_ln_g, a_ln_b, a_ws, a_bs, w_out, ln3_g, ln3_b, ffn2_w_up, ffn2_w_down):
    batch, seq, _ = x_prompt.shape
    n_samples = x_sample.shape[0]
    n_pages = page_table.shape[1]
    past = n_pages * PAGE_SIZE
    l = 0

    wgu1, wd1 = ffn1_w_up[l].astype(BF16), ffn1_w_down[l].astype(BF16)
    wgu2, wd2 = ffn2_w_up[l].astype(BF16), ffn2_w_down[l].astype(BF16)
    w_in_p = jnp.pad(w_in[l], ((0, 0), (0, IN_PAD - IN_WIDTH))).astype(BF16)
    w_out_p = w_out[l].reshape(2, A_WIDTH, D_MODEL).astype(BF16)
    causal = jnp.tril(jnp.ones((CHUNK, CHUNK), dtype=bool))
    ws_prompt = jnp.where(causal[None], a_ws[l], 0).astype(BF16)
    bias_prompt = jnp.repeat(a_bs[l].T, A_WIDTH // A_GROUPS, axis=1)
    eye = jnp.eye(CHUNK, dtype=F32)
    ws_sample = (a_ws[l][:, 0, 0][:, None, None] * eye[None]).astype(BF16)
    bias_sample = jnp.broadcast_to(bias_prompt[0:1], (CHUNK, A_WIDTH))

    pos_p = jnp.arange(seq, dtype=jnp.int32)
    pos_s = jnp.full((n_samples,), past, dtype=jnp.int32)
    tabs_p = _rope_tables(pos_p)
    tabs_s = _rope_tables(pos_s)

    xp = x_prompt.reshape(batch * seq, D_MODEL)
    x1p = _ffn_ln(xp, wgu1, wd1, ln1_g, ln1_b, tm=512)
    a_p, kt_p, vt_p, kit_p, q_p, qit_p, wit_p, kib_p, kdupt_p, vaug_p = _mixer_in(
        x1p, w_in_p, *tabs_p, a_ln_g, a_ln_b, ws_prompt, bias_prompt, batch, seq, tm=512, sample=False)
    b_p = _prompt_attention(q_p, qit_p, wit_p, kib_p, kdupt_p, vaug_p, batch, seq)
    yp = _mix_ffn_ln(a_p, b_p, x1p, w_out_p, ln2_g, ln2_b, wgu2, wd2, ln3_g, ln3_b, tm=512)

    xs = x_sample.reshape(n_samples, D_MODEL)
    x1s = _ffn_ln(xs, wgu1, wd1, ln1_g, ln1_b, tm=n_samples)
    a_s, kt_s, vt_s, kit_s, q_s, qi_s, kiwi_s, av_s = _mixer_in(
        x1s, w_in_p, *tabs_s, a_ln_g, a_ln_b, ws_sample, bias_sample, 1, n_samples, tm=n_samples, sample=True)
    pt_flat = page_table.reshape(-1)
    kidx_t = jnp.transpose(cache_kidx[l], (0, 2, 1))
    k_t = jnp.transpose(cache_k[l], (0, 2, 3, 1))
    v_t = jnp.transpose(cache_v[l], (0, 2, 3, 1))
    a_idx = jnp.pad(qi_s.reshape(n_samples, N_IDX_HEADS, IDX_DIM), ((0, 0), (0, 16 - N_IDX_HEADS), (0, 0)))
    w_idx = jnp.pad(kiwi_s[:, IDX_DIM:IDX_DIM + N_IDX_HEADS], ((0, 0), (0, 16 - N_IDX_HEADS)))[..., None]
    scores = _sample_scores(pt_flat, a_idx, w_idx, kidx_t, n_samples, n_pages)
    topk = min(TOPK_MAX, (past + 1) // 4)
    bias_s, bias_new = _sample_select(scores.reshape(n_samples, past), qi_s, kiwi_s, topk)
    aq = jnp.pad(q_s.reshape(n_samples, N_KV_HEADS, GROUP_SIZE, HEAD_DIM),
                 ((0, 0), (0, 0), (0, 16 - GROUP_SIZE), (0, 0)))
    knew = jnp.transpose(kt_s[0], (2, 0, 1))[:, :, None, :]
    vnew = jnp.transpose(vt_s[0], (2, 0, 1))[:, :, None, :]
    o_s = _sample_attention(pt_flat, aq, knew, vnew, bias_s.reshape(n_samples, 1, past),
                            bias_new.reshape(n_samples, 1, LANES), k_t, v_t, n_samples, n_pages)
    b_s = o_s[:, :, :GROUP_SIZE, :].reshape(n_samples, N_HEADS * HEAD_DIM).astype(BF16)
    ys = _mix_ffn_ln(a_s, b_s, x1s, w_out_p, ln2_g, ln2_b, wgu2, wd2, ln3_g, ln3_b, tm=n_samples)

    new_k_p = jnp.transpose(kt_p, (0, 3, 1, 2))[None]
    new_v_p = jnp.transpose(vt_p, (0, 3, 1, 2))[None]
    new_ki_p = jnp.transpose(kit_p, (0, 2, 1))[None]
    new_k_s = jnp.transpose(kt_s[0], (2, 0, 1))[None, :, None]
    new_v_s = jnp.transpose(vt_s[0], (2, 0, 1))[None, :, None]
    new_ki_s = jnp.transpose(kit_s[0], (1, 0))[None, :, None]
    return (yp.reshape(batch, seq, D_MODEL), ys.reshape(n_samples, 1, D_MODEL),
            new_k_p, new_v_p, new_ki_p, new_k_s, new_v_s, new_ki_s,
            av_s.reshape(1, n_samples, 1, A_WIDTH))
```

```python
import functools

import jax
import jax.numpy as jnp
from jax import lax
from jax.experimental import pallas as pl
from jax.experimental.pallas import tpu as pltpu

F32 = jnp.float32
BF16 = jnp.bfloat16

D_MODEL = 1024
D_FF = 2816
A_WIDTH = 512
A_GROUPS = 4
CHUNK = 128
HEAD_DIM = 64
N_HEADS = 8
N_KV_HEADS = 2
GROUP_SIZE = N_HEADS // N_KV_HEADS
N_IDX_HEADS = 4
IDX_DIM = 64
IDX_W_SCALE = (N_IDX_HEADS * IDX_DIM) ** -0.5
TOPK_MAX = 256
PAGE_SIZE = 128
ROPE_THETA = 10000.0
LN_EPS = 1e-5
DEPTH = 1
ALPHA = (2.0 * DEPTH) ** 0.25
ATTN_SCALE = HEAD_DIM ** -0.5

IN_SPLITS = (512, 512, 512, 128, 128, 256, 64, 4)
IN_WIDTH = sum(IN_SPLITS)
LANES = 128
IN_PAD = -(-IN_WIDTH // LANES) * LANES
COL_Q = 1024
COL_K = 1536

Q_BLOCK = 256
KEY_CHUNK = 256
CHUNKS_PER_TRIP = 4
FF_CHUNK = 256
N_FF_CHUNKS = D_FF // FF_CHUNK
VMEM_LIMIT = 56 * 1024 * 1024
INT_MIN = -2 ** 31
NEG_INF = float("-inf")


def _layernorm(y, g, b):
    mu = jnp.mean(y, axis=-1, keepdims=True)
    d = y - mu
    var = jnp.mean(d * d, axis=-1, keepdims=True)
    return d * lax.rsqrt(var + LN_EPS) * g + b


def _gelu_tanh(x):
    return 0.5 * x * (1.0 + jnp.tanh(0.7978845608028654 * (x + 0.044715 * (x * x * x))))


def _swiglu_ln(x, wup_ref, wd_ref, g_ref, b_ref):
    xb = x.astype(BF16)
    acc = jnp.zeros(x.shape, F32)
    for c in range(N_FF_CHUNKS):
        cols = slice(c * FF_CHUNK, (c + 1) * FF_CHUNK)
        gate = jnp.dot(xb, wup_ref[:, cols], preferred_element_type=F32)
        up = jnp.dot(xb, wup_ref[:, D_FF + c * FF_CHUNK:D_FF + (c + 1) * FF_CHUNK], preferred_element_type=F32)
        h = (gate * (1.0 / (1.0 + jnp.exp(-gate))) * up).astype(BF16)
        acc = acc + jnp.dot(h, wd_ref[cols, :], preferred_element_type=F32)
    return _layernorm(ALPHA * x + 0.5 * acc, g_ref[...], b_ref[...])


def _ffn_ln_kernel(x_ref, wgu_ref, wd_ref, g_ref, b_ref, o_ref):
    o_ref[...] = _swiglu_ln(x_ref[...], wgu_ref, wd_ref, g_ref, b_ref)


def _mix_ffn_ln_kernel(a_ref, b_ref, x_ref, wo_ref, g2_ref, b2_ref, wgu_ref, wd_ref, g3_ref, b3_ref, o_ref):
    mix = (jnp.dot(a_ref[...], wo_ref[0], preferred_element_type=F32)
           + jnp.dot(b_ref[...], wo_ref[1], preferred_element_type=F32))
    x2 = _layernorm(ALPHA * x_ref[...] + mix, g2_ref[...], b2_ref[...])
    o_ref[...] = _swiglu_ln(x2, wgu_ref, wd_ref, g3_ref, b3_ref)


def _resident(shape):
    return pl.BlockSpec(shape, lambda i: (0,) * len(shape), pipeline_mode=pl.Buffered(1))


def _ffn_ln(x, wgu, wd, g, b, tm):
    n = x.shape[0]
    row = lambda i: (i, 0)
    return pl.pallas_call(
        _ffn_ln_kernel,
        grid=(n // tm,),
        in_specs=[
            pl.BlockSpec((tm, D_MODEL), row),
            _resident((D_MODEL, 2 * D_FF)),
            _resident((D_FF, D_MODEL)),
            _resident((1, D_MODEL)),
            _resident((1, D_MODEL)),
        ],
        out_specs=pl.BlockSpec((tm, D_MODEL), row),
        out_shape=jax.ShapeDtypeStruct((n, D_MODEL), F32),
        compiler_params=pltpu.CompilerParams(
            dimension_semantics=("arbitrary",), vmem_limit_bytes=VMEM_LIMIT),
        name="ffn_ln",
    )(x, wgu, wd, g, b)


def _mix_ffn_ln(a, b, x, wo, g2, b2, wgu, wd, g3, b3, tm):
    n = x.shape[0]
    row = lambda i: (i, 0)
    return pl.pallas_call(
        _mix_ffn_ln_kernel,
        grid=(n // tm,),
        in_specs=[
            pl.BlockSpec((tm, A_WIDTH), row),
            pl.BlockSpec((tm, N_HEADS * HEAD_DIM), row),
            pl.BlockSpec((tm, D_MODEL), row),
            _resident((2, A_WIDTH, D_MODEL)),
            _resident((1, D_MODEL)),
            _resident((1, D_MODEL)),
            _resident((D_MODEL, 2 * D_FF)),
            _resident((D_FF, D_MODEL)),
            _resident((1, D_MODEL)),
            _resident((1, D_MODEL)),
        ],
        out_specs=pl.BlockSpec((tm, D_MODEL), row),
        out_shape=jax.ShapeDtypeStruct((n, D_MODEL), F32),
        compiler_params=pltpu.CompilerParams(
            dimension_semantics=("arbitrary",), vmem_limit_bytes=VMEM_LIMIT),
        name="mix_ffn_ln",
    )(a, b, x, wo, g2, b2, wgu, wd, g3, b3)


def _mixer_in_kernel(x_ref, w_ref, cos_ref, sin_ref, cosk_ref, sink_ref, ag_ref, ab_ref, ws_ref,
                     bias_ref, aout_ref, kt_ref, vt_ref, kit_ref, *mode_refs, sample):
    tm = x_ref.shape[0]
    xb = x_ref[...].astype(BF16)

    def proj(lo, hi):
        return jnp.dot(xb, w_ref[:, lo:hi], preferred_element_type=F32)

    u = _gelu_tanh(proj(0, A_WIDTH))
    vn = _layernorm(_gelu_tanh(proj(A_WIDTH, 2 * A_WIDTH)), ag_ref[...], ab_ref[...])
    vnb = vn.astype(BF16)
    for c in range(tm // CHUNK):
        rows = slice(c * CHUNK, (c + 1) * CHUNK)
        for g in range(A_GROUPS):
            cols = slice(g * LANES, (g + 1) * LANES)
            mixed = jnp.dot(ws_ref[g], vnb[rows, cols], preferred_element_type=F32) + bias_ref[:, cols]
            aout_ref[rows, cols] = (u[rows, cols] * mixed).astype(BF16)

    lane = lax.broadcasted_iota(jnp.int32, (tm, LANES), 1)
    first_half = (lane & (HEAD_DIM - 1)) < (HEAD_DIM // 2)

    def rope(x, cos, sin):
        partner = jnp.where(first_half, pltpu.roll(x, LANES - HEAD_DIM // 2, 1),
                            pltpu.roll(x, HEAD_DIM // 2, 1))
        return x * cos + partner * sin

    cos = cos_ref[...]
    sin = sin_ref[...]
    qall = proj(COL_Q, COL_K)
    rest = proj(COL_K, IN_PAD)
    k = rope(rest[:, 0:128], cos, sin)
    v = rest[:, 128:256]
    kiwi = rope(rest[:, 512:640], cosk_ref[...], sink_ref[...])
    k_t = k.T
    v_t = v.T
    kiwi_t = kiwi.T
    kt_ref[0] = k_t.reshape(N_KV_HEADS, HEAD_DIM, tm)
    vt_ref[0] = v_t.reshape(N_KV_HEADS, HEAD_DIM, tm)
    kit_ref[0] = kiwi_t[0:IDX_DIM]
    if sample:
        q_ref, qi_ref, kiwi_ref, av_ref = mode_refs
        av_ref[...] = vn
        kiwi_ref[...] = kiwi
        for j in range(4):
            cols = slice(j * LANES, (j + 1) * LANES)
            q_ref[:, cols] = (rope(qall[:, cols], cos, sin) * ATTN_SCALE).astype(BF16)
        for j in range(2):
            cols = slice(j * LANES, (j + 1) * LANES)
            qi_ref[:, cols] = rope(rest[:, 256 + j * LANES:256 + (j + 1) * LANES], cos, sin).astype(BF16)
    else:
        q_ref, qit_ref, wit_ref, kib_ref, kdupt_ref, vaug_ref = mode_refs
        for j in range(4):
            cols = slice(j * LANES, (j + 1) * LANES)
            q_ref[:, cols] = (rope(qall[:, cols], cos, sin) * ATTN_SCALE).astype(BF16)
        for j in range(2):
            cols = slice(j * LANES, (j + 1) * LANES)
            qit_ref[0, cols, :] = rope(rest[:, 256 + j * LANES:256 + (j + 1) * LANES], cos, sin).T.astype(BF16)
        wit_ref[0] = kiwi_t[IDX_DIM:IDX_DIM + 8]
        kib_ref[...] = kiwi.astype(BF16)
        lo = lane < HEAD_DIM
        k_tb = k_t.astype(BF16)
        for g in range(N_KV_HEADS):
            kg = k_tb[g * HEAD_DIM:(g + 1) * HEAD_DIM]
            kg2 = jnp.concatenate([kg, kg], axis=0)
            for cc in range(tm // KEY_CHUNK):
                kdupt_ref[0, g, cc] = kg2[:, cc * KEY_CHUNK:(cc + 1) * KEY_CHUNK]
        vaug_ref[:, 0:LANES] = jnp.where(lo, v, 1.0).astype(BF16)
        vaug_ref[:, LANES:2 * LANES] = jnp.where(lo, pltpu.roll(v, HEAD_DIM, 1), 1.0).astype(BF16)


def _mixer_in(x, w_in, cos, sin, cosk, sink, ag, ab, ws, bias, batch, seq, tm, sample):
    n = batch * seq
    nj = seq // tm
    row = lambda b, j: (b * nj + j, 0)
    tab = lambda b, j: (j, 0)
    const2 = lambda b, j: (0, 0)
    const3 = lambda b, j: (0, 0, 0)
    fmajor = lambda b, j: (b, 0, j)
    n_q = N_HEADS * HEAD_DIM
    n_qi = N_IDX_HEADS * IDX_DIM
    out_shape = [
        jax.ShapeDtypeStruct((n, A_WIDTH), BF16),
        jax.ShapeDtypeStruct((batch, N_KV_HEADS, HEAD_DIM, seq), F32),
        jax.ShapeDtypeStruct((batch, N_KV_HEADS, HEAD_DIM, seq), F32),
        jax.ShapeDtypeStruct((batch, IDX_DIM, seq), F32),
    ]
    out_specs = [
        pl.BlockSpec((tm, A_WIDTH), row),
        pl.BlockSpec((1, N_KV_HEADS, HEAD_DIM, tm), lambda b, j: (b, 0, 0, j)),
        pl.BlockSpec((1, N_KV_HEADS, HEAD_DIM, tm), lambda b, j: (b, 0, 0, j)),
        pl.BlockSpec((1, IDX_DIM, tm), fmajor),
    ]
    if sample:
        out_shape += [
            jax.ShapeDtypeStruct((n, n_q), BF16),
            jax.ShapeDtypeStruct((n, n_qi), BF16),
            jax.ShapeDtypeStruct((n, LANES), F32),
            jax.ShapeDtypeStruct((n, A_WIDTH), F32),
        ]
        out_specs += [
            pl.BlockSpec((tm, n_q), row),
            pl.BlockSpec((tm, n_qi), row),
            pl.BlockSpec((tm, LANES), row),
            pl.BlockSpec((tm, A_WIDTH), row),
        ]
    else:
        out_shape += [
            jax.ShapeDtypeStruct((n, n_q), BF16),
            jax.ShapeDtypeStruct((batch, n_qi, seq), BF16),
            jax.ShapeDtypeStruct((batch, 8, seq), F32),
            jax.ShapeDtypeStruct((n, LANES), BF16),
            jax.ShapeDtypeStruct((batch, N_KV_HEADS, seq // KEY_CHUNK, LANES, KEY_CHUNK), BF16),
            jax.ShapeDtypeStruct((n, N_KV_HEADS * LANES), BF16),
        ]
        out_specs += [
            pl.BlockSpec((tm, n_q), row),
            pl.BlockSpec((1, n_qi, tm), fmajor),
            pl.BlockSpec((1, 8, tm), fmajor),
            pl.BlockSpec((tm, LANES), row),
            pl.BlockSpec((1, N_KV_HEADS, tm // KEY_CHUNK, LANES, KEY_CHUNK), lambda b, j: (b, 0, j, 0, 0)),
            pl.BlockSpec((tm, N_KV_HEADS * LANES), row),
        ]
    return pl.pallas_call(
        functools.partial(_mixer_in_kernel, sample=sample),
        grid=(batch, nj),
        in_specs=[
            pl.BlockSpec((tm, D_MODEL), row),
            pl.BlockSpec((D_MODEL, IN_PAD), const2),
            pl.BlockSpec((tm, LANES), tab),
            pl.BlockSpec((tm, LANES), tab),
            pl.BlockSpec((tm, LANES), tab),
            pl.BlockSpec((tm, LANES), tab),
            pl.BlockSpec((1, A_WIDTH), const2),
            pl.BlockSpec((1, A_WIDTH), const2),
            pl.BlockSpec((A_GROUPS, CHUNK, CHUNK), const3),
            pl.BlockSpec((CHUNK, A_WIDTH), const2),
        ],
        out_specs=out_specs,
        out_shape=out_shape,
        compiler_params=pltpu.CompilerParams(
            dimension_semantics=("arbitrary", "arbitrary"), vmem_limit_bytes=VMEM_LIMIT),
        name="mixer_in",
    )(x, w_in, cos, sin, cosk, sink, ag, ab, ws, bias)


KEY_OF_NEG_INF = -2139095041
F32_MIN_NORMAL = 2.0 ** -126


def _sortable_key_by_position(s, pos):
    bits = pltpu.bitcast(s, jnp.int32)
    key = jnp.where(bits < 0, bits ^ jnp.int32(0x7FFFFFFF), bits)
    return jnp.where(jnp.abs(s) < F32_MIN_NORMAL, -1 - pos, key)


def _kth_largest_key(count_ge, rows, k):
    kf = jnp.float32(k)
    zero = jnp.zeros((rows, 1), jnp.int32)
    cur = jnp.where(count_ge(zero) >= kf, zero, jnp.int32(INT_MIN))

    def body(it, cur):
        cand = cur | jnp.left_shift(jnp.int32(1), jnp.int32(30) - it)
        return jnp.where(count_ge(cand) >= kf, cand, cur)

    return lax.fori_loop(0, 31, body, cur)


def _upper_tri_ones():
    r = lax.broadcasted_iota(jnp.int32, (LANES, LANES), 0)
    c = lax.broadcasted_iota(jnp.int32, (LANES, LANES), 1)
    return jnp.where(r <= c, 1.0, 0.0).astype(BF16)


def _tie_break_bias(key_ref, bias_ref, tau, need, n_cols):
    tri = _upper_tri_ones()
    rows = tau.shape[0]
    run = jnp.zeros((rows, 1), F32)
    for c in range(n_cols // LANES):
        cols = slice(c * LANES, (c + 1) * LANES)
        kc = key_ref[:, cols]
        eq = jnp.where(kc == tau, 1.0, 0.0)
        rank = run + jnp.dot(eq.astype(BF16), tri, preferred_element_type=F32)
        keep = (kc > tau) | ((kc == tau) & (rank <= need))
        bias_ref[:, cols] = jnp.where(keep, 0.0, NEG_INF)
        run = run + jnp.sum(eq, axis=-1, keepdims=True)
    return run


INT16_MIN = -2 ** 15
PACK16 = 16
N_COUNT_ACCS = 8


def _halves(ref, r):
    return ref[r * PACK16:(r + 1) * PACK16, :]


def _count_rows_16(mask_of_group, n_rows, lanes):
    one = jnp.ones((PACK16, lanes), jnp.int16)
    zero = jnp.zeros((PACK16, lanes), jnp.int16)
    accs = [zero] * min(N_COUNT_ACCS, n_rows // PACK16)
    for r in range(n_rows // PACK16):
        accs[r % len(accs)] = accs[r % len(accs)] + jnp.where(mask_of_group(r), one, zero)
    parts = accs
    while len(parts) > 1:
        parts = [parts[j] + parts[j + 1] for j in range(0, len(parts), 2)]
    return jnp.sum(parts[0].astype(jnp.int32), axis=0, keepdims=True)


def _bcast16(x, lanes):
    return jnp.broadcast_to(x, (PACK16, lanes)).astype(jnp.int16)


def _kth_largest_16(ref, n_rows, k):
    lanes = ref.shape[1]

    def body(it, cur):
        cand = cur + jnp.left_shift(jnp.int32(1), jnp.int32(15) - it)
        c16 = _bcast16(cand, lanes)
        cnt = _count_rows_16(lambda r: _halves(ref, r) >= c16, n_rows, lanes)
        return jnp.where(cnt >= k, cand, cur)

    return lax.fori_loop(0, 16, body, jnp.full((1, lanes), INT16_MIN, jnp.int32))


def _kth_largest_score_rows(score_ref, n_rows, k):
    lanes = score_ref.shape[1]
    kf = jnp.float32(k)

    def score_of_key(c):
        return pltpu.bitcast(jnp.where(c < 0, c ^ jnp.int32(0x7FFFFFFF), c), F32)

    def count_ge(c):
        return jnp.sum(jnp.where(score_ref[0:n_rows, :] >= score_of_key(c), 1.0, 0.0), axis=0, keepdims=True)

    zero = jnp.zeros((1, lanes), jnp.int32)
    cur = jnp.where(count_ge(zero) >= kf, zero, jnp.int32(INT_MIN))

    def body(it, cur):
        cand = cur | jnp.left_shift(jnp.int32(1), jnp.int32(30) - it)
        return jnp.where(count_ge(cand) >= kf, cand, cur)

    return score_of_key(lax.fori_loop(0, 31, body, cur))


def _kth_largest_key_rows(hi_ref, lo_ref, n_rows, k):
    lanes = hi_ref.shape[1]
    tau_hi = _kth_largest_16(hi_ref, n_rows, jnp.int32(k))
    t16 = _bcast16(tau_hi, lanes)
    n_above = _count_rows_16(lambda r: _halves(hi_ref, r) > t16, n_rows, lanes)
    min16 = jnp.full((PACK16, lanes), INT16_MIN, jnp.int16)
    for r in range(n_rows // PACK16):
        lo_ref[r * PACK16:(r + 1) * PACK16, :] = jnp.where(_halves(hi_ref, r) == t16, _halves(lo_ref, r), min16)
    tau_lo = _kth_largest_16(lo_ref, n_rows, jnp.int32(k) - n_above)
    return jnp.left_shift(tau_hi, 16) | (tau_lo - INT16_MIN)


def _lower_tri_ones():
    r = lax.broadcasted_iota(jnp.int32, (LANES, LANES), 0)
    c = lax.broadcasted_iota(jnp.int32, (LANES, LANES), 1)
    return jnp.where(c <= r, 1.0, 0.0).astype(BF16)


def _for_each_chunk(n_chunks, body):
    def trip(t, carry):
        for j in range(CHUNKS_PER_TRIP):
            body(CHUNKS_PER_TRIP * t + j)
        return carry

    lax.fori_loop(0, n_chunks // CHUNKS_PER_TRIP, trip, 0)
    done = (n_chunks // CHUNKS_PER_TRIP) * CHUNKS_PER_TRIP
    size = CHUNKS_PER_TRIP // 2
    while size >= 1:
        take = ((n_chunks - done) & size) != 0

        @pl.when(take)
        def _(done=done, size=size):
            for j in range(size):
                body(done + j)

        done = done + jnp.where(take, size, 0)
        size //= 2


def _chunk_rows(c):
    return pl.ds(pl.multiple_of(c * KEY_CHUNK, KEY_CHUNK), KEY_CHUNK)


def _select_topk_bias(i, n_chunks, qit_ref, wit_ref, kib_ref, key_ref, hi_ref, lo_ref, tau_ref, cnt_ref, biasq_ref):
    tq = Q_BLOCK
    zeros64 = jnp.zeros((HEAD_DIM, tq), BF16)
    qi_rhs = jnp.concatenate(
        [jnp.concatenate([qit_ref[0, h * IDX_DIM:(h + 1) * IDX_DIM, :], zeros64], axis=0)
         for h in range(N_IDX_HEADS)], axis=1)
    wi = wit_ref[0]
    q_pos = i * tq + lax.broadcasted_iota(jnp.int32, (KEY_CHUNK, tq), 1)
    row_iota = lax.broadcasted_iota(jnp.int32, (KEY_CHUNK, tq), 0)

    def keys_of(c):
        dots = jnp.dot(kib_ref[_chunk_rows(c), :], qi_rhs, preferred_element_type=F32)
        s = jnp.zeros((KEY_CHUNK, tq), F32)
        for h in range(N_IDX_HEADS):
            s = s + wi[h:h + 1, :] * jnp.maximum(dots[:, h * tq:(h + 1) * tq], 0.0)
        key_pos = c * KEY_CHUNK + row_iota
        key_ref[_chunk_rows(c), :] = jnp.where(key_pos <= q_pos, jnp.where(s == 0.0, 0.0, s), NEG_INF)

    _for_each_chunk(n_chunks, keys_of)

    for c in range(1, key_ref.shape[0] // KEY_CHUNK):
        @pl.when(i == c)
        def _(c=c):
            tau_c = _kth_largest_score_rows(key_ref, (c + 1) * KEY_CHUNK, TOPK_MAX)
            tau_ref[...] = jnp.broadcast_to(tau_c, tau_ref.shape)

    tau = tau_ref[0:1, :]

    cnt_ref[...] = jnp.zeros(cnt_ref.shape, F32)

    def bias_of(c):
        key = key_ref[_chunk_rows(c), :]
        ge = key >= tau
        biasq_ref[c] = jnp.where(ge, 0.0, NEG_INF).T
        cnt_ref[0:1, :] += jnp.sum(jnp.where(ge, 1.0, 0.0), axis=0, keepdims=True)
        cnt_ref[1:2, :] += jnp.sum(jnp.where(key > tau, 1.0, 0.0), axis=0, keepdims=True)

    _for_each_chunk(n_chunks, bias_of)
    n_ge = cnt_ref[0:1, :]
    n_gt = cnt_ref[1:2, :]
    zero = jnp.zeros((1, tq), F32)

    @pl.when(jnp.max(n_ge) > jnp.float32(TOPK_MAX))
    def _():
        need = jnp.float32(TOPK_MAX) - n_gt
        tri = _lower_tri_ones()

        def tie_of(c, run):
            halves = []
            for half in range(KEY_CHUNK // LANES):
                rows = pl.ds(pl.multiple_of(c * KEY_CHUNK + half * LANES, LANES), LANES)
                kc = key_ref[rows, :]
                eq = jnp.where(kc == tau, 1.0, 0.0)
                rank = run + jnp.dot(tri, eq.astype(BF16), preferred_element_type=F32)
                keep = (kc > tau) | ((kc == tau) & (rank <= need))
                halves.append(jnp.where(keep, 0.0, NEG_INF))
                run = run + jnp.sum(eq, axis=0, keepdims=True)
            biasq_ref[c] = jnp.concatenate(halves, axis=0).T
            return run

        lax.fori_loop(0, n_chunks, tie_of, zero)


def _prompt_attn_kernel(q_ref, qit_ref, wit_ref, kib_ref, kdupt_ref, vaug_ref, o_ref,
                        key_ref, hi_ref, lo_ref, tau_ref, cnt_ref, biasq_ref, lg_ref, m_ref, acc_ref):
    i = pl.program_id(1)
    n_chunks = i + 1
    tq = Q_BLOCK

    @pl.when(i == 0)
    def _():
        k_pos = lax.broadcasted_iota(jnp.int32, (tq, KEY_CHUNK), 1)
        q_pos = lax.broadcasted_iota(jnp.int32, (tq, KEY_CHUNK), 0)
        biasq_ref[0] = jnp.where(k_pos <= q_pos, 0.0, NEG_INF)

    @pl.when(i > 0)
    def _():
        _select_topk_bias(i, n_chunks, qit_ref, wit_ref, kib_ref, key_ref, hi_ref, lo_ref, tau_ref, cnt_ref,
                          biasq_ref)

    lane = lax.broadcasted_iota(jnp.int32, (tq, LANES), 1)
    lo = lane < HEAD_DIM
    q4 = []
    for g in range(N_KV_HEADS):
        parts = []
        for j in range(GROUP_SIZE // 2):
            blk = q_ref[:, g * 2 * LANES + j * LANES:g * 2 * LANES + (j + 1) * LANES].astype(F32)
            parts.append(jnp.where(lo, blk, 0.0).astype(BF16))
            parts.append(jnp.where(lo, 0.0, blk).astype(BF16))
        q4.append(jnp.concatenate(parts, axis=0))
    m_ref[...] = jnp.full(m_ref.shape, NEG_INF, F32)

    def mask_and_max(c):
        bias = biasq_ref[c]
        for g in range(N_KV_HEADS):
            raw = jnp.dot(q4[g], kdupt_ref[0, g, c], preferred_element_type=F32)
            for r in range(GROUP_SIZE):
                rows = slice(r * tq, (r + 1) * tq)
                lg = raw[rows] + bias
                lg_ref[g, c, rows, :] = lg
                m_ref[g, rows, :] = jnp.maximum(m_ref[g, rows, :], jnp.maximum(lg[:, :LANES], lg[:, LANES:]))

    _for_each_chunk(n_chunks, mask_and_max)
    m_ref[...] = jnp.broadcast_to(jnp.max(m_ref[...], axis=-1, keepdims=True), m_ref.shape)
    acc_ref[...] = jnp.zeros(acc_ref.shape, F32)

    def accumulate(c):
        for g in range(N_KV_HEADS):
            m = m_ref[g]
            lg = lg_ref[g, c]
            p = jnp.concatenate([jnp.exp(lg[:, :LANES] - m), jnp.exp(lg[:, LANES:] - m)], axis=1).astype(BF16)
            acc_ref[g] += jnp.dot(p, vaug_ref[_chunk_rows(c), g * LANES:(g + 1) * LANES],
                                  preferred_element_type=F32)

    _for_each_chunk(n_chunks, accumulate)
    for g in range(N_KV_HEADS):
        acc = acc_ref[g]
        out = acc * (1.0 / pltpu.roll(acc, HEAD_DIM, 1))
        for j in range(GROUP_SIZE // 2):
            even = out[(2 * j) * tq:(2 * j + 1) * tq]
            odd = pltpu.roll(out[(2 * j + 1) * tq:(2 * j + 2) * tq], HEAD_DIM, 1)
            o_ref[:, g * 2 * LANES + j * LANES:g * 2 * LANES + (j + 1) * LANES] = jnp.where(
                lo, even, odd).astype(BF16)


def _prompt_attention(q, qit, wit, kib, kdupt, vaug, batch, seq):
    nb = seq // Q_BLOCK
    n_q = N_HEADS * HEAD_DIM
    n_qi = N_IDX_HEADS * IDX_DIM
    n_kc = seq // KEY_CHUNK
    qblk = lambda b, i: (b, 0, i)
    perb = lambda b, i: (b, 0)
    return pl.pallas_call(
        _prompt_attn_kernel,
        grid=(batch, nb),
        in_specs=[
            pl.BlockSpec((Q_BLOCK, n_q), lambda b, i: (b * nb + i, 0)),
            pl.BlockSpec((1, n_qi, Q_BLOCK), qblk),
            pl.BlockSpec((1, 8, Q_BLOCK), qblk),
            pl.BlockSpec((seq, LANES), perb),
            pl.BlockSpec((1, N_KV_HEADS, n_kc, LANES, KEY_CHUNK), lambda b, i: (b, 0, 0, 0, 0)),
            pl.BlockSpec((seq, N_KV_HEADS * LANES), perb),
        ],
        out_specs=pl.BlockSpec((Q_BLOCK, n_q), lambda b, i: (b * nb + i, 0)),
        out_shape=jax.ShapeDtypeStruct((batch * seq, n_q), BF16),
        scratch_shapes=[
            pltpu.VMEM((seq, Q_BLOCK), F32),
            pltpu.VMEM((seq, Q_BLOCK), jnp.int16),
            pltpu.VMEM((seq, Q_BLOCK), jnp.int16),
            pltpu.VMEM((8, Q_BLOCK), F32),
            pltpu.VMEM((8, Q_BLOCK), F32),
            pltpu.VMEM((n_kc, Q_BLOCK, KEY_CHUNK), F32),
            pltpu.VMEM((N_KV_HEADS, n_kc, GROUP_SIZE * Q_BLOCK, KEY_CHUNK), F32),
            pltpu.VMEM((N_KV_HEADS, GROUP_SIZE * Q_BLOCK, LANES), F32),
            pltpu.VMEM((N_KV_HEADS, GROUP_SIZE * Q_BLOCK, LANES), F32),
        ],
        compiler_params=pltpu.CompilerParams(
            dimension_semantics=("arbitrary", "arbitrary"), vmem_limit_bytes=VMEM_LIMIT),
        name="prompt_attention",
    )(q, qit, wit, kib, kdupt, vaug)


def _page_copies(pt_ref, b, n_pages, src_hbm, dst_for_page, sem):
    return [pltpu.make_async_copy(src_hbm.at[pt_ref[b * n_pages + p]], dst_for_page(p), sem)
            for p in range(n_pages)]


def _sample_scores_kernel(pt_ref, a_ref, w_ref, kidx_hbm, o_ref, buf, sem, *, n_pages):
    b = pl.program_id(0)
    nb = pl.num_programs(0)
    n_slots = buf.shape[0]
    slot = b % n_slots

    def copies(bb, sl):
        return _page_copies(pt_ref, bb, n_pages, kidx_hbm,
                            lambda p: buf.at[sl, :, pl.ds(p * PAGE_SIZE, PAGE_SIZE)], sem.at[sl])

    @pl.when(b == 0)
    def _():
        for cp in copies(0, 0) + copies(1, 1):
            cp.start()

    for cp in copies(b, slot):
        cp.wait()
    ahead = copies(jnp.minimum(b + 2, nb - 1), (b + 2) % n_slots)
    for cp in ahead:
        cp.start()
    dots = jnp.dot(a_ref[0], buf[slot].astype(BF16), preferred_element_type=F32)
    o_ref[0] = jnp.sum(jnp.maximum(dots, 0.0) * w_ref[0], axis=0, keepdims=True)

    @pl.when(b == nb - 1)
    def _():
        for cp in copies(nb - 1, (b + 1) % n_slots) + ahead:
            cp.wait()


def _sample_scores(pt_flat, a, w, kidx_t, n_samples, n_pages):
    past = n_pages * PAGE_SIZE
    return pl.pallas_call(
        functools.partial(_sample_scores_kernel, n_pages=n_pages),
        grid_spec=pltpu.PrefetchScalarGridSpec(
            num_scalar_prefetch=1,
            grid=(n_samples,),
            in_specs=[
                pl.BlockSpec((1, 16, IDX_DIM), lambda b, pt: (b, 0, 0)),
                pl.BlockSpec((1, 16, 1), lambda b, pt: (b, 0, 0)),
                pl.BlockSpec(memory_space=pl.ANY),
            ],
            out_specs=pl.BlockSpec((1, 1, past), lambda b, pt: (b, 0, 0)),
            scratch_shapes=[pltpu.VMEM((3, IDX_DIM, past), F32), pltpu.SemaphoreType.DMA((3,))],
        ),
        out_shape=jax.ShapeDtypeStruct((n_samples, 1, past), F32),
        compiler_params=pltpu.CompilerParams(
            dimension_semantics=("arbitrary",), vmem_limit_bytes=VMEM_LIMIT),
        name="sample_scores",
    )(pt_flat, a, w, kidx_t)


def _sample_select_kernel(s_ref, qi_ref, kiwi_ref, bias_ref, bias_new_ref, score_ref, *, topk):
    rows, past = s_ref.shape
    lane = lax.broadcasted_iota(jnp.int32, (rows, LANES), 1)
    lo = lane < IDX_DIM
    kiwi = kiwi_ref[...]
    kidup = jnp.where(lo, kiwi, pltpu.roll(kiwi, IDX_DIM, 1)).astype(BF16).astype(F32)
    s_new = jnp.zeros((rows, 1), F32)
    for j in range(2):
        t = qi_ref[:, j * LANES:(j + 1) * LANES].astype(F32) * kidup
        for half in range(2):
            d = jnp.sum(jnp.where(lo if half == 0 else jnp.logical_not(lo), t, 0.0), axis=-1, keepdims=True)
            h = 2 * j + half
            s_new = s_new + kiwi[:, IDX_DIM + h:IDX_DIM + h + 1] * jnp.maximum(d, 0.0)
    score_new = jnp.where(s_new == 0.0, 0.0, s_new)
    score_ref[...] = jnp.where(s_ref[...] == 0.0, 0.0, s_ref[...])

    def score_of_key(c):
        return pltpu.bitcast(jnp.where(c < 0, c ^ jnp.int32(0x7FFFFFFF), c), F32)

    def count_ge(cf):
        return (jnp.sum(jnp.where(score_ref[...] >= cf, 1.0, 0.0), axis=-1, keepdims=True)
                + jnp.where(score_new >= cf, 1.0, 0.0))

    tau = score_of_key(_kth_largest_key(lambda c: count_ge(score_of_key(c)), rows, topk))
    score = score_ref[...]
    bias_ref[...] = jnp.where(score >= tau, 0.0, NEG_INF)
    n_gt = (jnp.sum(jnp.where(score > tau, 1.0, 0.0), axis=-1, keepdims=True)
            + jnp.where(score_new > tau, 1.0, 0.0))
    need = jnp.float32(topk) - n_gt
    bias_new_ref[...] = jnp.broadcast_to(jnp.where(score_new >= tau, 0.0, NEG_INF), (rows, LANES))

    @pl.when(jnp.max(count_ge(tau)) > jnp.float32(topk))
    def _():
        n_eq_past = _tie_break_bias(score_ref, bias_ref, tau, need, past)
        keep_new = (score_new > tau) | ((score_new == tau) & (n_eq_past + 1.0 <= need))
        bias_new_ref[...] = jnp.broadcast_to(jnp.where(keep_new, 0.0, NEG_INF), (rows, LANES))


def _sample_select(scores, qi, kiwi, topk):
    rows, past = scores.shape
    return pl.pallas_call(
        functools.partial(_sample_select_kernel, topk=topk),
        out_shape=[jax.ShapeDtypeStruct((rows, past), F32), jax.ShapeDtypeStruct((rows, LANES), F32)],
        scratch_shapes=[pltpu.VMEM((rows, past), F32)],
        compiler_params=pltpu.CompilerParams(vmem_limit_bytes=VMEM_LIMIT),
        name="sample_select",
    )(scores, qi, kiwi)


def _sample_attn_kernel(pt_ref, aq_ref, knew_ref, vnew_ref, bias_ref, bias_new_ref, k_hbm, v_hbm, o_ref,
                        kbuf, vbuf, sem, *, n_pages):
    b = pl.program_id(0)
    nb = pl.num_programs(0)
    slot = b % 2

    def copies(bb, sl):
        dst_k = lambda p: kbuf.at[sl, :, :, pl.ds(p * PAGE_SIZE, PAGE_SIZE)]
        dst_v = lambda p: vbuf.at[sl, :, :, pl.ds(p * PAGE_SIZE, PAGE_SIZE)]
        return (_page_copies(pt_ref, bb, n_pages, k_hbm, dst_k, sem.at[0, sl])
                + _page_copies(pt_ref, bb, n_pages, v_hbm, dst_v, sem.at[1, sl]))

    @pl.when(b == 0)
    def _():
        for cp in copies(b, slot):
            cp.start()

    @pl.when(b + 1 < nb)
    def _():
        for cp in copies(b + 1, 1 - slot):
            cp.start()

    for cp in copies(b, slot):
        cp.wait()
    bias = bias_ref[0]
    bias_new = bias_new_ref[0][:, 0:1]
    for g in range(N_KV_HEADS):
        aq = aq_ref[0, g]
        logits = jnp.dot(aq, kbuf[slot, g].astype(BF16), preferred_element_type=F32) + bias
        knew = knew_ref[0, g].astype(BF16).astype(F32)
        l_new = jnp.sum(aq.astype(F32) * knew, axis=-1, keepdims=True) + bias_new
        m = jnp.maximum(jnp.max(logits, axis=-1, keepdims=True), l_new)
        p = jnp.exp(logits - m)
        p_new = jnp.exp(l_new - m)
        l = jnp.sum(p, axis=-1, keepdims=True) + p_new
        pv = lax.dot_general(p.astype(BF16), vbuf[slot, g].astype(BF16), (((1,), (1,)), ((), ())),
                             preferred_element_type=F32)
        pv = pv + p_new.astype(BF16).astype(F32) * vnew_ref[0, g].astype(BF16).astype(F32)
        o_ref[0, g] = pv * (1.0 / l)


def _sample_attention(pt_flat, aq, knew, vnew, bias, bias_new, k_t, v_t, n_samples, n_pages):
    past = n_pages * PAGE_SIZE
    blk4 = lambda b, pt: (b, 0, 0, 0)
    blk3 = lambda b, pt: (b, 0, 0)
    return pl.pallas_call(
        functools.partial(_sample_attn_kernel, n_pages=n_pages),
        grid_spec=pltpu.PrefetchScalarGridSpec(
            num_scalar_prefetch=1,
            grid=(n_samples,),
            in_specs=[
                pl.BlockSpec((1, N_KV_HEADS, 16, HEAD_DIM), blk4),
                pl.BlockSpec((1, N_KV_HEADS, 1, HEAD_DIM), blk4),
                pl.BlockSpec((1, N_KV_HEADS, 1, HEAD_DIM), blk4),
                pl.BlockSpec((1, 1, past), blk3),
                pl.BlockSpec((1, 1, LANES), blk3),
                pl.BlockSpec(memory_space=pl.ANY),
                pl.BlockSpec(memory_space=pl.ANY),
            ],
            out_specs=pl.BlockSpec((1, N_KV_HEADS, 16, HEAD_DIM), blk4),
            scratch_shapes=[
                pltpu.VMEM((2, N_KV_HEADS, HEAD_DIM, past), F32),
                pltpu.VMEM((2, N_KV_HEADS, HEAD_DIM, past), F32),
                pltpu.SemaphoreType.DMA((2, 2)),
            ],
        ),
        out_shape=jax.ShapeDtypeStruct((n_samples, N_KV_HEADS, 16, HEAD_DIM), F32),
        compiler_params=pltpu.CompilerParams(
            dimension_semantics=("arbitrary",), vmem_limit_bytes=VMEM_LIMIT),
        name="sample_attention",
    )(pt_flat, aq, knew, vnew, bias, bias_new, k_t, v_t)


def _rope_tables(pos):
    half = HEAD_DIM // 2
    inv = ROPE_THETA ** (-jnp.arange(half, dtype=F32) / half)
    ang = pos.astype(F32)[:, None] * inv[None, :]
    c = jnp.cos(ang)
    s = jnp.sin(ang)
    cos64 = jnp.concatenate([c, c], axis=1)
    sin64 = jnp.concatenate([-s, s], axis=1)
    n = pos.shape[0]
    cos = jnp.concatenate([cos64, cos64], axis=1)
    sin = jnp.concatenate([sin64, sin64], axis=1)
    cosk = jnp.concatenate([cos64, jnp.full((n, LANES - IDX_DIM), IDX_W_SCALE, F32)], axis=1)
    sink = jnp.concatenate([sin64, jnp.zeros((n, LANES - IDX_DIM), F32)], axis=1)
    return cos, sin, cosk, sink


def kernel(x_prompt, x_sample, cache_k, cache_v, cache_kidx, page_table, ln1_g, ln1_b, ffn1_w_up, ffn1_w_down, ln2_g, ln2_b, w_in, a_ln_g, a_ln_b, a_ws, a_bs, w_out, ln3_g, ln3_b, ffn2_w_up, ffn2_w_down):
    batch, seq, _ = x_prompt.shape
    n_samples = x_sample.shape[0]
    n_pages = page_table.shape[1]
    past = n_pages * PAGE_SIZE
    l = 0

    wgu1, wd1 = ffn1_w_up[l].astype(BF16), ffn1_w_down[l].astype(BF16)
    wgu2, wd2 = ffn2_w_up[l].astype(BF16), ffn2_w_down[l].astype(BF16)
    w_in_p = jnp.pad(w_in[l], ((0, 0), (0, IN_PAD - IN_WIDTH))).astype(BF16)
    w_out_p = w_out[l].reshape(2, A_WIDTH, D_MODEL).astype(BF16)
    causal = jnp.tril(jnp.ones((CHUNK, CHUNK), dtype=bool))
    ws_prompt = jnp.where(causal[None], a_ws[l], 0).astype(BF16)
    bias_prompt = jnp.repeat(a_bs[l].T, A_WIDTH // A_GROUPS, axis=1)
    eye = jnp.eye(CHUNK, dtype=F32)
    ws_sample = (a_ws[l][:, 0, 0][:, None, None] * eye[None]).astype(BF16)
    bias_sample = jnp.broadcast_to(bias_prompt[0:1], (CHUNK, A_WIDTH))

    pos_p = jnp.arange(seq, dtype=jnp.int32)
    pos_s = jnp.full((n_samples,), past, dtype=jnp.int32)
    tabs_p = _rope_tables(pos_p)
    tabs_s = _rope_tables(pos_s)

    xp = x_prompt.reshape(batch * seq, D_MODEL)
    x1p = _ffn_ln(xp, wgu1, wd1, ln1_g, ln1_b, tm=512)
    a_p, kt_p, vt_p, kit_p, q_p, qit_p, wit_p, kib_p, kdupt_p, vaug_p = _mixer_in(
        x1p, w_in_p, *tabs_p, a_ln_g, a_ln_b, ws_prompt, bias_prompt, batch, seq, tm=512, sample=False)
    b_p = _prompt_attention(q_p, qit_p, wit_p, kib_p, kdupt_p, vaug_p, batch, seq)
    yp = _mix_ffn_ln(a_p, b_p, x1p, w_out_p, ln2_g, ln2_b, wgu2, wd2, ln3_g, ln3_b, tm=512)

    xs = x_sample.reshape(n_samples, D_MODEL)
    x1s = _ffn_ln(xs, wgu1, wd1, ln1_g, ln1_b, tm=n_samples)
    a_s, kt_s, vt_s, kit_s, q_s, qi_s, kiwi_s, av_s = _mixer_in(
        x1s, w_in_p, *tabs_s, a_ln_g, a_ln_b, ws_sample, bias_sample, 1, n_samples, tm=n_samples, sample=True)
    pt_flat = page_table.reshape(-1)
    kidx_t = jnp.transpose(cache_kidx[l], (0, 2, 1))
    k_t = jnp.transpose(cache_k[l], (0, 2, 3, 1))
    v_t = jnp.transpose(cache_v[l], (0, 2, 3, 1))
    a_idx = jnp.pad(qi_s.reshape(n_samples, N_IDX_HEADS, IDX_DIM), ((0, 0), (0, 16 - N_IDX_HEADS), (0, 0)))
    w_idx = jnp.pad(kiwi_s[:, IDX_DIM:IDX_DIM + N_IDX_HEADS], ((0, 0), (0, 16 - N_IDX_HEADS)))[..., None]
    scores = _sample_scores(pt_flat, a_idx, w_idx, kidx_t, n_samples, n_pages)
    topk = min(TOPK_MAX, (past + 1) // 4)
    bias_s, bias_new = _sample_select(scores.reshape(n_samples, past), qi_s, kiwi_s, topk)
    aq = jnp.pad(q_s.reshape(n_samples, N_KV_HEADS, GROUP_SIZE, HEAD_DIM),
                 ((0, 0), (0, 0), (0, 16 - GROUP_SIZE), (0, 0)))
    knew = jnp.transpose(kt_s[0], (2, 0, 1))[:, :, None, :]
    vnew = jnp.transpose(vt_s[0], (2, 0, 1))[:, :, None, :]
    o_s = _sample_attention(pt_flat, aq, knew, vnew, bias_s.reshape(n_samples, 1, past),
                            bias_new.reshape(n_samples, 1, LANES), k_t, v_t, n_samples, n_pages)
    b_s = o_s[:, :, :GROUP_SIZE, :].reshape(n_samples, N_HEADS * HEAD_DIM).astype(BF16)
    ys = _mix_ffn_ln(a_s, b_s, x1s, w_out_p, ln2_g, ln2_b, wgu2, wd2, ln3_g, ln3_b, tm=n_samples)

    new_k_p = jnp.transpose(kt_p, (0, 3, 1, 2))[None]
    new_v_p = jnp.transpose(vt_p, (0, 3, 1, 2))[None]
    new_ki_p = jnp.transpose(kit_p, (0, 2, 1))[None]
    new_k_s = jnp.transpose(kt_s[0], (2, 0, 1))[None, :, None]
    new_v_s = jnp.transpose(vt_s[0], (2, 0, 1))[None, :, None]
    new_ki_s = jnp.transpose(kit_s[0], (1, 0))[None, :, None]
    return (yp.reshape(batch, seq, D_MODEL), ys.reshape(n_samples, 1, D_MODEL),
            new_k_p, new_v_p, new_ki_p, new_k_s, new_v_s, new_ki_s,
            av_s.reshape(1, n_samples, 1, A_WIDTH))
```
